```python
import math
import jax, jax.numpy as jnp
from jax import lax
import numpy as np

D_MODEL = 1024
BATCH = 2
SEQ = 8192
DEPTH = 4
DEC_BATCH = 32
DEC_SEQ = 4
PAST_LEN = 8192
PAGE_SIZE = 128

N_EVEN = (DEPTH + 1) // 2
N_ODD = DEPTH // 2
W_BRANCH = D_MODEL // 2
W_A = W_BRANCH
CONV_A = 31
W_B = W_BRANCH
GMLP_CHUNK = 128
B_GROUPS = 4
B_GROUP_DIM = W_B // B_GROUPS
W_C = W_BRANCH
CONV_C = 4
C_BLOCKS = 8
C_BLOCK_DIM = W_C // C_BLOCKS
LRU_C = 8.0
H_D = 8
HD_D = 64
W_D = H_D * HD_D
MOBA_BLOCK = 256
MOBA_TOPK = 3
MOBA_Q_CHUNK = 64
ROPE_THETA = 10000.0

EVEN_IN = 3 * W_A + 3 * W_B
ODD_IN = 2 * W_C + 4 * W_D
DN_ALPHA = (2.0 * DEPTH) ** 0.25
DN_BETA = (8.0 * DEPTH) ** -0.25
LN_EPS = 1e-5

kernel_name = "hybrid_conv_gmlp_rglru_moba_decode_step"


def layer_norm(x, g, b):
    xf = x.astype(jnp.float32)
    mu = jnp.mean(xf, axis=-1, keepdims=True)
    var = jnp.mean(jnp.square(xf - mu), axis=-1, keepdims=True)
    return ((xf - mu) * lax.rsqrt(var + LN_EPS)).astype(x.dtype) * g + b


def causal_depthwise_conv(x, buf, w, b):
    k_w = w.shape[0]
    xp = jnp.concatenate([buf.astype(x.dtype), x], axis=1)
    y = lax.conv_general_dilated(xp, w[:, None, :].astype(x.dtype), window_strides=(1,), padding='VALID',
                                 dimension_numbers=('NWC', 'WIO', 'NWC'), feature_group_count=x.shape[-1])
    return y + b, xp[:, xp.shape[1] - (k_w - 1):]


def rope(x, pos):
    half = HD_D // 2
    inv = ROPE_THETA ** (-jnp.arange(half, dtype=jnp.float32) / half)
    ang = pos.astype(jnp.float32)[:, None] * inv[None, :]
    cos = jnp.cos(ang)[None, :, None, :]
    sin = jnp.sin(ang)[None, :, None, :]
    xf = x.astype(jnp.float32)
    x1, x2 = xf[..., :half], xf[..., half:]
    return jnp.concatenate([x1 * cos - x2 * sin, x2 * cos + x1 * sin], axis=-1).astype(x.dtype)


def rglru_scan(a, bx, h0):
    def step(h, ab):
        a_t, b_t = ab
        h = a_t * h + b_t
        return h, h
    h_last, hs = lax.scan(step, h0.astype(jnp.float32), (jnp.swapaxes(a, 0, 1), jnp.swapaxes(bx, 0, 1)))
    return jnp.swapaxes(hs, 0, 1), h_last


def moba_attention(q, k, v, pos0):
    bsz, tq, nh, hd = q.shape
    tk = k.shape[1]
    nb = -(-tk // MOBA_BLOCK)
    pad = nb * MOBA_BLOCK - tk
    kb = jnp.pad(k, ((0, 0), (0, pad), (0, 0), (0, 0))).reshape(bsz, nb, MOBA_BLOCK, nh, hd)
    vb = jnp.pad(v, ((0, 0), (0, pad), (0, 0), (0, 0))).reshape(bsz, nb, MOBA_BLOCK, nh, hd)
    k_mean = jnp.mean(kb.astype(jnp.float32), axis=2)
    n_sel = min(MOBA_TOPK, nb)
    qc_len = min(tq, MOBA_Q_CHUNK)
    nqc = tq // qc_len
    qc = q.reshape(bsz, nqc, qc_len, nh, hd).transpose(1, 0, 3, 2, 4)
    scale = hd ** -0.5
    bi = jnp.arange(bsz)[:, None, None, None]
    hi = jnp.arange(nh)[None, :, None, None]

    def chunk(args):
        q_c, c = args
        start = pos0 + c * qc_len
        qpos = start + jnp.arange(qc_len)
        own = start // MOBA_BLOCK
        qf = q_c.astype(jnp.float32)
        gate = jnp.einsum('bhqd,bnhd->bhqn', qf, k_mean)
        gate = jnp.where(jnp.arange(nb) < own, gate, -jnp.inf)
        top_val, top_idx = lax.top_k(gate, n_sel)
        valid = jnp.isfinite(top_val)
        ks = kb[bi, top_idx, :, hi].astype(jnp.float32)
        vs = vb[bi, top_idx, :, hi].astype(jnp.float32)
        s_sel = jnp.einsum('bhqd,bhqnkd->bhqnk', qf, ks) * scale
        s_sel = jnp.where(valid[..., None], s_sel, -jnp.inf).reshape(bsz, nh, qc_len, n_sel * MOBA_BLOCK)
        k_own = lax.dynamic_index_in_dim(kb, own, axis=1, keepdims=False).astype(jnp.float32)
        v_own = lax.dynamic_index_in_dim(vb, own, axis=1, keepdims=False).astype(jnp.float32)
        kpos = own * MOBA_BLOCK + jnp.arange(MOBA_BLOCK)
        s_own = jnp.einsum('bhqd,bkhd->bhqk', qf, k_own) * scale
        s_own = jnp.where(kpos[None, :] <= qpos[:, None], s_own, -jnp.inf)
        p = jax.nn.softmax(jnp.concatenate([s_sel, s_own], axis=-1), axis=-1)
        p_sel = p[..., :n_sel * MOBA_BLOCK].reshape(bsz, nh, qc_len, n_sel, MOBA_BLOCK)
        out = jnp.einsum('bhqnk,bhqnkd->bhqd', p_sel, vs) + jnp.einsum('bhqk,bkhd->bhqd', p[..., n_sel * MOBA_BLOCK:], v_own)
        return out.astype(q.dtype)

    out = lax.map(chunk, (qc, jnp.arange(nqc)))
    return out.transpose(1, 0, 3, 2, 4).reshape(bsz, tq, nh, hd)


def even_layer(x, conv_buf, w_in, conv_w, conv_b, ln_a_g, ln_a_b, ln_v_g, ln_v_b, gm_w, gm_b, w_out, ln_g, ln_b):
    bsz, t, _ = x.shape
    z = x @ w_in
    a_val, a_glu, a_gate, b_u, b_v, b_gate = jnp.split(
        z, [W_A, 2 * W_A, 3 * W_A, 3 * W_A + W_B, 3 * W_A + 2 * W_B], axis=-1)
    a_in = a_val * jax.nn.sigmoid(a_glu)
    a_conv, new_buf = causal_depthwise_conv(a_in, conv_buf, conv_w, conv_b)
    a_out = jax.nn.silu(layer_norm(a_conv, ln_a_g, ln_a_b)) * jax.nn.silu(a_gate)
    u = jax.nn.gelu(b_u)
    v = layer_norm(jax.nn.gelu(b_v), ln_v_g, ln_v_b)
    tc = min(t, GMLP_CHUNK)
    nc = t // tc
    mask = jnp.tril(jnp.ones((tc, tc), dtype=bool))
    w_mix = jnp.where(mask, gm_w[:, :tc, :tc], 0.0).astype(x.dtype)
    vc = v.reshape(bsz, nc, tc, B_GROUPS, B_GROUP_DIM)
    mixed = jnp.einsum('gts,bnsgc->bntgc', w_mix, vc) + gm_b[:, :tc].T[None, None, :, :, None]
    b_out = (u.reshape(vc.shape) * mixed).reshape(bsz, t, W_B) * jax.nn.silu(b_gate)
    y = jnp.concatenate([a_out, b_out], axis=-1) @ w_out
    return layer_norm(DN_ALPHA * x + y, ln_g, ln_b), new_buf, v


def odd_layer(x, conv_buf, h0, past_k, past_v, pos0, w_in, conv_w, conv_b, wr, br, wi, bi_, lam, w_out, ln_g, ln_b):
    bsz, t, _ = x.shape
    z = x @ w_in
    c_x, c_gate, q, k, v, d_gate = jnp.split(
        z, [W_C, 2 * W_C, 2 * W_C + W_D, 2 * W_C + 2 * W_D, 2 * W_C + 3 * W_D], axis=-1)
    xc, new_buf = causal_depthwise_conv(c_x, conv_buf, conv_w, conv_b)
    xb = xc.reshape(bsz, t, C_BLOCKS, C_BLOCK_DIM)
    r = jax.nn.sigmoid(jnp.einsum('btnd,nde->btne', xb, wr).reshape(bsz, t, W_C) + br)
    ig = jax.nn.sigmoid(jnp.einsum('btnd,nde->btne', xb, wi).reshape(bsz, t, W_C) + bi_)
    log_a = -LRU_C * r.astype(jnp.float32) * jax.nn.softplus(-lam.astype(jnp.float32))
    a = jnp.exp(log_a)
    bx = jnp.sqrt(-jnp.expm1(2.0 * log_a)) * (ig * xc).astype(jnp.float32)
    h_all, h_last = rglru_scan(a, bx, h0)
    c_out = h_all.astype(x.dtype) * jax.nn.silu(c_gate)
    pos = pos0 + jnp.arange(t)
    q = rope(q.reshape(bsz, t, H_D, HD_D), pos)
    k = rope(k.reshape(bsz, t, H_D, HD_D), pos)
    v = v.reshape(bsz, t, H_D, HD_D)
    if past_k is None:
        k_all, v_all = k, v
    else:
        k_all = jnp.concatenate([past_k.astype(k.dtype), k], axis=1)
        v_all = jnp.concatenate([past_v.astype(v.dtype), v], axis=1)
    att = moba_attention(q, k_all, v_all, pos0)
    d_out = att.reshape(bsz, t, W_D) * jax.nn.silu(d_gate)
    y = jnp.concatenate([c_out, d_out], axis=-1) @ w_out
    return layer_norm(DN_ALPHA * x + y, ln_g, ln_b), new_buf, h_last.astype(h0.dtype), k, v


def setup_inputs(seed: int = 0) -> dict:
    key = jax.random.key(seed)
    ks = jax.random.split(key, 40)
    f32 = jnp.float32
    n_pages = PAST_LEN // PAGE_SIZE
    n_pool = (DEC_BATCH * n_pages * 5) // 4
    nrm = lambda k, shape, s: jax.random.normal(k, shape, f32) * s
    gain = lambda k, shape: 1.0 + 0.02 * jax.random.normal(k, shape, f32)
    perm = jax.random.permutation(ks[5], n_pool)
    page_table = perm[:DEC_BATCH * n_pages].reshape(DEC_BATCH, n_pages).astype(jnp.int32)
    a_init = jax.random.uniform(ks[30], (N_ODD, W_C), f32, 0.9, 0.999)
    return {
        "x_prompt": nrm(ks[0], (BATCH, SEQ, D_MODEL), 1.0),
        "x_sample": nrm(ks[1], (DEC_BATCH, DEC_SEQ, D_MODEL), 1.0),
        "cache_k": nrm(ks[2], (n_pool, N_ODD, PAGE_SIZE, H_D, HD_D), 1.0),
        "cache_v": nrm(ks[3], (n_pool, N_ODD, PAGE_SIZE, H_D, HD_D), 1.0),
        "page_table": page_table,
        "state_conv_a": nrm(ks[4], (DEC_BATCH, N_EVEN, CONV_A - 1, W_A), 1.0),
        "state_conv_c": nrm(ks[6], (DEC_BATCH, N_ODD, CONV_C - 1, W_C), 1.0),
        "state_lru_h": nrm(ks[7], (DEC_BATCH, N_ODD, W_C), 0.5),
        "w_in_even": nrm(ks[8], (N_EVEN, D_MODEL, EVEN_IN), D_MODEL ** -0.5),
        "conv_a_w": nrm(ks[9], (N_EVEN, CONV_A, W_A), CONV_A ** -0.5),
        "conv_a_b": nrm(ks[10], (N_EVEN, W_A), 0.02),
        "ln_a_g": gain(ks[11], (N_EVEN, W_A)),
        "ln_a_b": nrm(ks[12], (N_EVEN, W_A), 0.02),
        "ln_v_g": gain(ks[13], (N_EVEN, W_B)),
        "ln_v_b": nrm(ks[14], (N_EVEN, W_B), 0.02),
        "gmlp_w": nrm(ks[15], (N_EVEN, B_GROUPS, GMLP_CHUNK, GMLP_CHUNK), GMLP_CHUNK ** -0.5),
        "gmlp_b": gain(ks[16], (N_EVEN, B_GROUPS, GMLP_CHUNK)),
        "w_out_even": nrm(ks[17], (N_EVEN, W_A + W_B, D_MODEL), (W_A + W_B) ** -0.5 * DN_BETA),
        "post_g_even": gain(ks[18], (N_EVEN, D_MODEL)),
        "post_b_even": nrm(ks[19], (N_EVEN, D_MODEL), 0.02),
        "w_in_odd": nrm(ks[20], (N_ODD, D_MODEL, ODD_IN), D_MODEL ** -0.5),
        "conv_c_w": nrm(ks[21], (N_ODD, CONV_C, W_C), CONV_C ** -0.5),
        "conv_c_b": nrm(ks[22], (N_ODD, W_C), 0.02),
        "lru_wr": nrm(ks[23], (N_ODD, C_BLOCKS, C_BLOCK_DIM, C_BLOCK_DIM), C_BLOCK_DIM ** -0.5),
        "lru_br": nrm(ks[24], (N_ODD, W_C), 0.02),
        "lru_wi": nrm(ks[25], (N_ODD, C_BLOCKS, C_BLOCK_DIM, C_BLOCK_DIM), C_BLOCK_DIM ** -0.5),
        "lru_bi": nrm(ks[26], (N_ODD, W_C), 0.02),
        "lru_lambda": jnp.log(a_init) - jnp.log1p(-a_init),
        "w_out_odd": nrm(ks[27], (N_ODD, W_C + W_D, D_MODEL), (W_C + W_D) ** -0.5 * DN_BETA),
        "post_g_odd": gain(ks[28], (N_ODD, D_MODEL)),
        "post_b_odd": nrm(ks[29], (N_ODD, D_MODEL), 0.02),
    }


def reference(x_prompt, x_sample, cache_k, cache_v, page_table, state_conv_a, state_conv_c, state_lru_h,
              w_in_even, conv_a_w, conv_a_b, ln_a_g, ln_a_b, ln_v_g, ln_v_b, gmlp_w, gmlp_b, w_out_even,
              post_g_even, post_b_even, w_in_odd, conv_c_w, conv_c_b, lru_wr, lru_br, lru_wi, lru_bi,
              lru_lambda, w_out_odd, post_g_odd, post_b_odd):
    xp, xs = x_prompt, x_sample
    conv_a_p, conv_a_s, gm_v_s = [], [], []
    conv_c_p, conv_c_s, h_p, h_s = [], [], [], []
    k_p, v_p, k_s, v_s = [], [], [], []
    for l in range(DEPTH):
        j = l // 2
        if l % 2 == 0:
            prm = (w_in_even[j], conv_a_w[j], conv_a_b[j], ln_a_g[j], ln_a_b[j], ln_v_g[j], ln_v_b[j],
                   gmlp_w[j], gmlp_b[j], w_out_even[j], post_g_even[j], post_b_even[j])
            xp, buf_p, _ = even_layer(xp, jnp.zeros((BATCH, CONV_A - 1, W_A), xp.dtype), *prm)
            xs, buf_s, vrows = even_layer(xs, state_conv_a[:, j], *prm)
            conv_a_p.append(buf_p)
            conv_a_s.append(buf_s)
            gm_v_s.append(vrows)
        else:
            prm = (w_in_odd[j], conv_c_w[j], conv_c_b[j], lru_wr[j], lru_br[j], lru_wi[j], lru_bi[j],
                   lru_lambda[j], w_out_odd[j], post_g_odd[j], post_b_odd[j])
            xp, cb_p, hl_p, kn_p, vn_p = odd_layer(
                xp, jnp.zeros((BATCH, CONV_C - 1, W_C), xp.dtype), jnp.zeros((BATCH, W_C), xp.dtype),
                None, None, 0, *prm)
            past_k = cache_k[page_table, j].reshape(DEC_BATCH, PAST_LEN, H_D, HD_D)
            past_v = cache_v[page_table, j].reshape(DEC_BATCH, PAST_LEN, H_D, HD_D)
            xs, cb_s, hl_s, kn_s, vn_s = odd_layer(
                xs, state_conv_c[:, j], state_lru_h[:, j], past_k, past_v, PAST_LEN, *prm)
            conv_c_p.append(cb_p)
            conv_c_s.append(cb_s)
            h_p.append(hl_p)
            h_s.append(hl_s)
            k_p.append(kn_p)
            v_p.append(vn_p)
            k_s.append(kn_s)
            v_s.append(vn_s)
    return (xp, xs,
            jnp.stack(conv_a_p, axis=1), jnp.stack(conv_a_s, axis=1), jnp.stack(gm_v_s, axis=1),
            jnp.stack(conv_c_p, axis=1), jnp.stack(conv_c_s, axis=1),
            jnp.stack(h_p, axis=1), jnp.stack(h_s, axis=1),
            jnp.stack(k_p, axis=1), jnp.stack(v_p, axis=1), jnp.stack(k_s, axis=1), jnp.stack(v_s, axis=1))
```

```python
import functools
import math

import jax
import jax.numpy as jnp
import numpy as np
from jax import lax
from jax.experimental import pallas as pl
from jax.experimental.pallas import tpu as pltpu

F32 = jnp.float32
BF16 = jnp.bfloat16

GMLP_CHUNK = 128
B_GROUPS = 4
C_BLOCKS = 8
LRU_C = 8.0
HD_D = 64
MOBA_BLOCK = 256
MOBA_TOPK = 3
ROPE_THETA = 10000.0
LN_EPS = 1e-5

LANES = 128
SUBLANES = 8
VMEM_LIMIT_BYTES = 56 * 1024 * 1024

MASK_NEG = -1e30
M_INIT = -1e29

PROMPT_TM = 512
CONV_HALO = 32
PAGES_PER_STEP = 8


def _ln(x, g, b):
    mu = jnp.mean(x, axis=-1, keepdims=True)
    xc = x - mu
    var = jnp.mean(xc * xc, axis=-1, keepdims=True)
    return xc * lax.rsqrt(var + LN_EPS) * g + b


def _sigmoid(x):
    return jax.nn.sigmoid(x)


def _silu(x):
    return x * jax.nn.sigmoid(x)


def _gelu(x):
    c = math.sqrt(2.0 / math.pi)
    return x * (0.5 * (1.0 + jnp.tanh(c * (x + 0.044715 * (x * x * x)))))


def _softplus(y):
    return jnp.maximum(y, 0.0) + jnp.log1p(jnp.exp(-jnp.abs(y)))


def _dot(a, b):
    return jnp.dot(a, b, preferred_element_type=F32)


def _dot_nt(a, b, precision=None):
    return lax.dot_general(a, b, (((1,), (1,)), ((), ())), preferred_element_type=F32, precision=precision)


def _cparams(sem):
    return pltpu.CompilerParams(dimension_semantics=sem, vmem_limit_bytes=VMEM_LIMIT_BYTES)


def _full(shape):
    n = len(shape)
    return pl.BlockSpec(shape, lambda *_: (0,) * n)


def _causal_conv_tile(scr, a_in, cw_ref, cb, first_tile, tm, kw):
    w = a_in.shape[1]

    @pl.when(first_tile)
    def _():
        scr[0:CONV_HALO, :] = jnp.zeros((CONV_HALO, w), F32)

    scr[CONV_HALO:CONV_HALO + tm, :] = a_in
    acc = jnp.zeros((tm, w), F32) + cb
    for k in range(kw):
        acc = acc + cw_ref[k:k + 1, :] * scr[pl.ds(CONV_HALO - (kw - 1) + k, tm), :]
    return acc


def _even_prompt_kernel(x_ref, win_ref, cw_ref, cb_ref, lag_ref, lab_ref, lvg_ref, lvb_ref, gmw_ref, gmb_ref,
                        wout_ref, pg_ref, pb_ref, y_ref, buf_ref, scr, *, alpha, tm, w, kw):
    t = pl.program_id(1)
    nt = pl.num_programs(1)
    x = x_ref[0]
    z = _dot(x.astype(BF16), win_ref[...])
    a_val, a_glu, a_gate = z[:, 0:w], z[:, w:2 * w], z[:, 2 * w:3 * w]
    b_u, b_v, b_gate = z[:, 3 * w:4 * w], z[:, 4 * w:5 * w], z[:, 5 * w:6 * w]

    a_in = a_val * _sigmoid(a_glu)
    a_conv = _causal_conv_tile(scr, a_in, cw_ref, cb_ref[...], t == 0, tm, kw)

    @pl.when(t == nt - 1)
    def _():
        buf_ref[0] = scr[pl.ds(CONV_HALO + tm - (kw - 1), kw - 1), :]

    scr[0:CONV_HALO, :] = scr[pl.ds(tm, CONV_HALO), :]
    a_out = _silu(_ln(a_conv, lag_ref[...], lab_ref[...])) * _silu(a_gate)

    u = _gelu(b_u)
    v = _ln(_gelu(b_v), lvg_ref[...], lvb_ref[...]).astype(BF16)
    gd = w // B_GROUPS
    ri = lax.broadcasted_iota(jnp.int32, (GMLP_CHUNK, GMLP_CHUNK), 0)
    ci = lax.broadcasted_iota(jnp.int32, (GMLP_CHUNK, GMLP_CHUNK), 1)
    wmix = [jnp.where(ri >= ci, gmw_ref[g], 0.0).astype(BF16) for g in range(B_GROUPS)]
    rows = []
    for c in range(tm // GMLP_CHUNK):
        vc = v[c * GMLP_CHUNK:(c + 1) * GMLP_CHUNK, :]
        rows.append(jnp.concatenate(
            [_dot(wmix[g], vc[:, g * gd:(g + 1) * gd]) for g in range(B_GROUPS)], axis=1) + gmb_ref[...])
    mixed = jnp.concatenate(rows, axis=0)
    b_out = u * mixed * _silu(b_gate)

    y = _dot(a_out.astype(BF16), wout_ref[0:w, :]) + _dot(b_out.astype(BF16), wout_ref[w:2 * w, :])
    y_ref[0] = _ln(alpha * x + y, pg_ref[...], pb_ref[...])


def _even_prompt(x, prm, alpha):
    (w_in, conv_w, conv_b, ln_a_g, ln_a_b, ln_v_g, ln_v_b, gm_w, gm_bias, w_out, pg, pb) = prm
    bsz, t, d = x.shape
    tm = min(PROMPT_TM, t)
    w = conv_w.shape[1]
    kw = conv_w.shape[0]
    assert t % tm == 0 and tm % GMLP_CHUNK == 0 and kw - 1 <= CONV_HALO <= tm
    kern = functools.partial(_even_prompt_kernel, alpha=alpha, tm=tm, w=w, kw=kw)
    return pl.pallas_call(
        kern,
        grid=(bsz, t // tm),
        in_specs=[pl.BlockSpec((1, tm, d), lambda b, i: (b, i, 0)),
                  _full(w_in.shape), _full(conv_w.shape), _full(conv_b.shape), _full(ln_a_g.shape),
                  _full(ln_a_b.shape), _full(ln_v_g.shape), _full(ln_v_b.shape), _full(gm_w.shape),
                  _full(gm_bias.shape), _full(w_out.shape), _full(pg.shape), _full(pb.shape)],
        out_specs=[pl.BlockSpec((1, tm, d), lambda b, i: (b, i, 0)),
                   pl.BlockSpec((1, kw - 1, w), lambda b, i: (b, 0, 0))],
        out_shape=[jax.ShapeDtypeStruct((bsz, t, d), F32), jax.ShapeDtypeStruct((bsz, kw - 1, w), F32)],
        scratch_shapes=[pltpu.VMEM((CONV_HALO + tm, w), F32)],
        compiler_params=_cparams(("arbitrary", "arbitrary")),
        name="even_prompt",
    )(x, w_in, conv_w, conv_b, ln_a_g, ln_a_b, ln_v_g, ln_v_b, gm_w, gm_bias, w_out, pg, pb)


def _rope(x, cos, sin_signed):
    n = x.shape[1]
    reps = n // LANES
    cos_f = jnp.concatenate([cos] * reps, axis=1)
    sin_f = jnp.concatenate([sin_signed] * reps, axis=1)
    lane = lax.broadcasted_iota(jnp.int32, x.shape, 1)
    first = (lane % HD_D) < (HD_D // 2)
    rot = jnp.where(first, pltpu.roll(x, n - HD_D // 2, 1), pltpu.roll(x, HD_D // 2, 1))
    return x * cos_f + rot * sin_f


def _lru_coeffs(xc, wg_ref, br, bi, lam, w):
    gates = _dot(xc.astype(BF16), wg_ref[...])
    r = _sigmoid(gates[:, 0:w] + br)
    ig = _sigmoid(gates[:, w:2 * w] + bi)
    log_a = (-LRU_C) * r * _softplus(-lam)
    a = jnp.exp(log_a)
    bx = jnp.sqrt(1.0 - a * a) * (ig * xc)
    return a, bx


def _scan_tile(a, b):
    tm = a.shape[0]
    row = lax.broadcasted_iota(jnp.int32, a.shape, 0)
    s = 1
    while s < tm:
        a_sh = pltpu.roll(a, s, 0)
        b_sh = pltpu.roll(b, s, 0)
        ok = row >= s
        b = jnp.where(ok, a * b_sh + b, b)
        a = jnp.where(ok, a * a_sh, a)
        s *= 2
    return a, b


def _odd_proj_prompt_kernel(x_ref, win_ref, cw_ref, cb_ref, wg_ref, br_ref, bi_ref, lam_ref, cos_ref, sin_ref,
                            cout_ref, q_ref, kt_ref, vt_ref, ktb_ref, vb_ref, dg_ref, kmean_ref, buf_ref, hl_ref,
                            scr, h_scr, *, tm, w, kw, scale):
    t = pl.program_id(1)
    nt = pl.num_programs(1)
    x = x_ref[0]
    z = _dot(x.astype(BF16), win_ref[...])
    c_x, c_gate = z[:, 0:w], z[:, w:2 * w]
    wd = (z.shape[1] - 2 * w) // 4
    o = 2 * w
    q, k, v, d_gate = z[:, o:o + wd], z[:, o + wd:o + 2 * wd], z[:, o + 2 * wd:o + 3 * wd], z[:, o + 3 * wd:o + 4 * wd]

    xc = _causal_conv_tile(scr, c_x, cw_ref, cb_ref[...], t == 0, tm, kw)

    @pl.when(t == nt - 1)
    def _():
        buf_ref[0] = scr[pl.ds(CONV_HALO + tm - (kw - 1), kw - 1), :]

    scr[0:CONV_HALO, :] = scr[pl.ds(tm, CONV_HALO), :]

    a, bx = _lru_coeffs(xc, wg_ref, br_ref[...], bi_ref[...], lam_ref[...], w)

    @pl.when(t == 0)
    def _():
        h_scr[...] = jnp.zeros_like(h_scr)

    a_cum, h_loc = _scan_tile(a, bx)
    h = h_loc + a_cum * h_scr[0:1, :]
    h_scr[0:1, :] = h[tm - 1:tm, :]

    @pl.when(t == nt - 1)
    def _():
        hl_ref[0] = h[tm - 1:tm, :]

    cout_ref[0] = (h * _silu(c_gate)).astype(BF16)

    cos = cos_ref[...]
    sin = sin_ref[...]
    q_r = _rope(q, cos, sin)
    k_r = _rope(k, cos, sin)
    q_ref[0] = q_r * scale
    k_t = k_r.T
    kt_ref[0] = k_t
    vt_ref[0] = v.T
    vb_ref[0] = v.astype(BF16)
    dg_ref[0] = _silu(d_gate)
    nblk = tm // MOBA_BLOCK
    for c in range(nblk):
        ktb_ref[0, c] = k_t[:, c * MOBA_BLOCK:(c + 1) * MOBA_BLOCK].astype(BF16)
        kmean_ref[0, pl.ds(t * nblk + c, 1), :] = jnp.mean(
            k_r[c * MOBA_BLOCK:(c + 1) * MOBA_BLOCK, :], axis=0, keepdims=True)


def _odd_proj_prompt(x, prm, cos, sin):
    (w_in, conv_w, conv_b, w_gates, br, bi, lam) = prm
    bsz, t, d = x.shape
    tm = min(PROMPT_TM, t)
    w = conv_w.shape[1]
    kw = conv_w.shape[0]
    wd = (w_in.shape[1] - 2 * w) // 4
    nb = t // MOBA_BLOCK
    assert t % tm == 0 and tm % MOBA_BLOCK == 0 and kw - 1 <= CONV_HALO <= tm
    kern = functools.partial(_odd_proj_prompt_kernel, tm=tm, w=w, kw=kw, scale=HD_D ** -0.5)
    row_blk = lambda n: pl.BlockSpec((1, tm, n), lambda b, i: (b, i, 0))
    per_b = lambda r, n: pl.BlockSpec((1, r, n), lambda b, i: (b, 0, 0))
    col_blk = pl.BlockSpec((1, wd, tm), lambda b, i: (b, 0, i))
    return pl.pallas_call(
        kern,
        grid=(bsz, t // tm),
        in_specs=[row_blk(d), _full(w_in.shape), _full(conv_w.shape), _full(conv_b.shape), _full(w_gates.shape),
                  _full(br.shape), _full(bi.shape), _full(lam.shape),
                  pl.BlockSpec((tm, LANES), lambda b, i: (i, 0)), pl.BlockSpec((tm, LANES), lambda b, i: (i, 0))],
        out_specs=[row_blk(w), row_blk(wd), col_blk, col_blk,
                   pl.BlockSpec((1, tm // MOBA_BLOCK, wd, MOBA_BLOCK), lambda b, i: (b, i, 0, 0)),
                   row_blk(wd), row_blk(wd), per_b(nb, wd), per_b(kw - 1, w), per_b(1, w)],
        out_shape=[jax.ShapeDtypeStruct((bsz, t, w), BF16),
                   jax.ShapeDtypeStruct((bsz, t, wd), F32),
                   jax.ShapeDtypeStruct((bsz, wd, t), F32),
                   jax.ShapeDtypeStruct((bsz, wd, t), F32),
                   jax.ShapeDtypeStruct((bsz, nb, wd, MOBA_BLOCK), BF16),
                   jax.ShapeDtypeStruct((bsz, t, wd), BF16),
                   jax.ShapeDtypeStruct((bsz, t, wd), F32),
                   jax.ShapeDtypeStruct((bsz, nb, wd), F32),
                   jax.ShapeDtypeStruct((bsz, kw - 1, w), F32),
                   jax.ShapeDtypeStruct((bsz, 1, w), F32)],
        scratch_shapes=[pltpu.VMEM((CONV_HALO + tm, w), F32), pltpu.VMEM((SUBLANES, w), F32)],
        compiler_params=_cparams(("arbitrary", "arbitrary")),
        name="odd_proj_prompt",
    )(x, w_in, conv_w, conv_b, w_gates, br, bi, lam, cos, sin)


def _select_topk_bias(gt, own):
    row = lax.broadcasted_iota(jnp.int32, gt.shape, 0)
    neg_inf = jnp.float32(-jnp.inf)
    g = jnp.where(row < own, gt, neg_inf)
    sel = row == own
    big = jnp.int32(gt.shape[0])
    for _ in range(MOBA_TOPK):
        mx = jnp.max(g, axis=0, keepdims=True)
        idx = jnp.min(jnp.where(g == mx, row, big), axis=0, keepdims=True)
        pick = (row == idx) & (mx > neg_inf)
        sel = sel | pick
        g = jnp.where(pick, neg_inf, g)
    return jnp.where(sel, 0.0, MASK_NEG)


def _moba_prompt_kernel(q_ref, kt_ref, v_ref, kmean_ref, dg_ref, o_ref, acc_ref, m_ref, *, nbp):
    i = pl.program_id(2)
    bq = MOBA_BLOCK
    q = q_ref[0]
    lane = lax.broadcasted_iota(jnp.int32, (bq, LANES), 1)
    head_lanes = [lane < HD_D, lane >= HD_D]
    one_lane = [lane == HD_D, lane == 0]
    kml = lax.broadcasted_iota(jnp.int32, (nbp, LANES), 1)
    km_heads = [kml < HD_D, kml >= HD_D]
    kmean = kmean_ref[0]

    qaug = []
    for h in range(2):
        gt = _dot_nt(jnp.where(km_heads[h], kmean, 0.0), q, precision=lax.Precision.HIGHEST)
        bias = _select_topk_bias(gt, i).T
        qh = jnp.where(head_lanes[h], q, 0.0).astype(BF16)
        qaug.append(jnp.concatenate([qh, bias.astype(BF16)], axis=1))

    acc_ref[...] = jnp.zeros_like(acc_ref)
    m_ref[...] = jnp.full(m_ref.shape, M_INIT, F32)
    oh_row = lax.broadcasted_iota(jnp.int32, (nbp, bq), 0)

    def block(n, extra_bias):
        start = pl.multiple_of(n * bq, bq)
        vn = v_ref[0, pl.ds(start, bq), :]
        kaug = jnp.concatenate([kt_ref[0, n], (oh_row == n).astype(BF16)], axis=0)
        for h in range(2):
            s = _dot(qaug[h], kaug)
            if extra_bias is not None:
                s = s + extra_bias
            m_prev = m_ref[h]
            m_new = jnp.maximum(m_prev, jnp.max(s, axis=1, keepdims=True))
            p = jnp.exp(s - jnp.concatenate([m_new] * (bq // LANES), axis=1))
            alpha = jnp.exp(m_prev - m_new)
            vh = jnp.where(head_lanes[h], vn, one_lane[h].astype(BF16))
            acc_ref[h] = alpha * acc_ref[h] + _dot(p.astype(BF16), vh)
            m_ref[h] = m_new

    def body(n, carry):
        block(n, None)
        return carry

    lax.fori_loop(0, i, body, 0)
    r = lax.broadcasted_iota(jnp.int32, (bq, bq), 0)
    c = lax.broadcasted_iota(jnp.int32, (bq, bq), 1)
    block(i, jnp.where(c <= r, 0.0, MASK_NEG))

    acc_a = acc_ref[0]
    acc_b = acc_ref[1]
    out = jnp.where(head_lanes[0], acc_a / acc_a[:, HD_D:HD_D + 1], acc_b / acc_b[:, 0:1])
    o_ref[0] = (out * dg_ref[0]).astype(BF16)


def _moba_prompt(q, ktb, vb, kmean, dg):
    bsz, t, wd = q.shape
    nq = t // MOBA_BLOCK
    npairs = wd // LANES
    nbp = kmean.shape[1]
    assert t % MOBA_BLOCK == 0 and wd % LANES == 0 and nbp % LANES == 0 and nq <= nbp
    kern = functools.partial(_moba_prompt_kernel, nbp=nbp)
    qblk = pl.BlockSpec((1, MOBA_BLOCK, LANES), lambda b, p, i: (b, i, p))
    seq = lambda r: pl.BlockSpec((1, r, LANES), lambda b, p, i: (b, 0, p))
    ktblk = pl.BlockSpec((1, nq, LANES, MOBA_BLOCK), lambda b, p, i: (b, 0, p, 0))
    return pl.pallas_call(
        kern,
        grid=(bsz, npairs, nq),
        in_specs=[qblk, ktblk, seq(t), seq(nbp), qblk],
        out_specs=qblk,
        out_shape=jax.ShapeDtypeStruct((bsz, t, wd), BF16),
        scratch_shapes=[pltpu.VMEM((2, MOBA_BLOCK, LANES), F32), pltpu.VMEM((2, MOBA_BLOCK, LANES), F32)],
        compiler_params=_cparams(("arbitrary", "arbitrary", "arbitrary")),
        name="moba_prompt",
    )(q, ktb, vb, kmean, dg)


def _out_proj_kernel(x_ref, c_ref, d_ref, wout_ref, pg_ref, pb_ref, y_ref, *, alpha, w):
    x = x_ref[0]
    y = _dot(c_ref[0], wout_ref[0:w, :]) + _dot(d_ref[0], wout_ref[w:, :])
    y_ref[0] = _ln(alpha * x + y, pg_ref[...], pb_ref[...])


def _out_proj(x, c_out, d_out, w_out, pg, pb, alpha):
    bsz, t, d = x.shape
    tm = min(PROMPT_TM, t)
    w = c_out.shape[2]
    assert t % tm == 0
    kern = functools.partial(_out_proj_kernel, alpha=alpha, w=w)
    blk = lambda n: pl.BlockSpec((1, tm, n), lambda b, i: (b, i, 0))
    return pl.pallas_call(
        kern,
        grid=(bsz, t // tm),
        in_specs=[blk(d), blk(w), blk(d_out.shape[2]), _full(w_out.shape), _full(pg.shape), _full(pb.shape)],
        out_specs=blk(d),
        out_shape=jax.ShapeDtypeStruct((bsz, t, d), F32),
        compiler_params=_cparams(("arbitrary", "arbitrary")),
        name="out_proj",
    )(x, c_out, d_out, w_out, pg, pb)


def _sample_conv(state_ref, a_in, cw_ref, cb, ns, ts, kw):
    def xp(s):
        return state_ref[s] if s < kw - 1 else a_in[(s - (kw - 1)) * ns:(s - (kw - 2)) * ns, :]

    out = []
    for t in range(ts):
        acc = jnp.zeros((ns, a_in.shape[1]), F32) + cb
        for k in range(kw):
            acc = acc + cw_ref[k:k + 1, :] * xp(t + k)
        out.append(acc)
    return jnp.concatenate(out, axis=0), [xp(ts + s) for s in range(kw - 1)]


def _even_sample_kernel(x_ref, st_ref, win_ref, cw_ref, cb_ref, lag_ref, lab_ref, lvg_ref, lvb_ref, wexp_ref,
                        bexp_ref, wout_ref, pg_ref, pb_ref, y_ref, buf_ref, gv_ref, *, alpha, ns, ts, w, kw):
    x = x_ref[...]
    z = _dot(x.astype(BF16), win_ref[...])
    a_val, a_glu, a_gate = z[:, 0:w], z[:, w:2 * w], z[:, 2 * w:3 * w]
    b_u, b_v, b_gate = z[:, 3 * w:4 * w], z[:, 4 * w:5 * w], z[:, 5 * w:6 * w]
    a_in = a_val * _sigmoid(a_glu)
    a_conv, new_state = _sample_conv(st_ref, a_in, cw_ref, cb_ref[...], ns, ts, kw)
    for s in range(kw - 1):
        buf_ref[s] = new_state[s]
    a_out = _silu(_ln(a_conv, lag_ref[...], lab_ref[...])) * _silu(a_gate)

    u = _gelu(b_u)
    v = _ln(_gelu(b_v), lvg_ref[...], lvb_ref[...])
    gv_ref[...] = v
    mixed = []
    for t in range(ts):
        acc = jnp.zeros((ns, w), F32)
        for s in range(t + 1):
            acc = acc + wexp_ref[t * ts + s:t * ts + s + 1, :] * v[s * ns:(s + 1) * ns, :]
        mixed.append(acc + bexp_ref[t:t + 1, :])
    b_out = u * jnp.concatenate(mixed, axis=0) * _silu(b_gate)
    y = _dot(a_out.astype(BF16), wout_ref[0:w, :]) + _dot(b_out.astype(BF16), wout_ref[w:2 * w, :])
    y_ref[...] = _ln(alpha * x + y, pg_ref[...], pb_ref[...])


def _even_sample(x, state, prm, alpha, ns, ts):
    (w_in, conv_w, conv_b, ln_a_g, ln_a_b, ln_v_g, ln_v_b, wexp, bexp, w_out, pg, pb) = prm
    r, d = x.shape
    w = conv_w.shape[1]
    kw = conv_w.shape[0]
    kern = functools.partial(_even_sample_kernel, alpha=alpha, ns=ns, ts=ts, w=w, kw=kw)
    args = (x, state, w_in, conv_w, conv_b, ln_a_g, ln_a_b, ln_v_g, ln_v_b, wexp, bexp, w_out, pg, pb)
    return pl.pallas_call(
        kern,
        grid=(1,),
        in_specs=[_full(a.shape) for a in args],
        out_specs=[_full((r, d)), _full((kw - 1, ns, w)), _full((r, w))],
        out_shape=[jax.ShapeDtypeStruct((r, d), F32), jax.ShapeDtypeStruct((kw - 1, ns, w), F32),
                   jax.ShapeDtypeStruct((r, w), F32)],
        compiler_params=_cparams(("arbitrary",)),
        name="even_sample",
    )(*args)


def _odd_proj_sample_kernel(x_ref, st_ref, h0_ref, win_ref, cw_ref, cb_ref, wg_ref, br_ref, bi_ref, lam_ref,
                            cos_ref, sin_ref, cout_ref, q_ref, k_ref, v_ref, dg_ref, buf_ref, hl_ref,
                            *, ns, ts, w, kw, scale):
    x = x_ref[...]
    z = _dot(x.astype(BF16), win_ref[...])
    c_x, c_gate = z[:, 0:w], z[:, w:2 * w]
    wd = (z.shape[1] - 2 * w) // 4
    o = 2 * w
    q, k, v, d_gate = z[:, o:o + wd], z[:, o + wd:o + 2 * wd], z[:, o + 2 * wd:o + 3 * wd], z[:, o + 3 * wd:o + 4 * wd]
    xc, new_state = _sample_conv(st_ref, c_x, cw_ref, cb_ref[...], ns, ts, kw)
    for s in range(kw - 1):
        buf_ref[s] = new_state[s]
    a, bx = _lru_coeffs(xc, wg_ref, br_ref[...], bi_ref[...], lam_ref[...], w)
    h = h0_ref[...]
    hs = []
    for t in range(ts):
        h = a[t * ns:(t + 1) * ns, :] * h + bx[t * ns:(t + 1) * ns, :]
        hs.append(h)
    hl_ref[...] = h
    cout_ref[...] = (jnp.concatenate(hs, axis=0) * _silu(c_gate)).astype(BF16)
    cos = cos_ref[...]
    sin = sin_ref[...]
    q_ref[...] = _rope(q, cos, sin) * scale
    k_ref[...] = _rope(k, cos, sin)
    v_ref[...] = v
    dg_ref[...] = _silu(d_gate)


def _odd_proj_sample(x, state, h0, prm, cos, sin, ns, ts):
    (w_in, conv_w, conv_b, w_gates, br, bi, lam) = prm
    r, d = x.shape
    w = conv_w.shape[1]
    kw = conv_w.shape[0]
    wd = (w_in.shape[1] - 2 * w) // 4
    kern = functools.partial(_odd_proj_sample_kernel, ns=ns, ts=ts, w=w, kw=kw, scale=HD_D ** -0.5)
    args = (x, state, h0, w_in, conv_w, conv_b, w_gates, br, bi, lam, cos, sin)
    sd = jax.ShapeDtypeStruct
    return pl.pallas_call(
        kern,
        grid=(1,),
        in_specs=[_full(a.shape) for a in args],
        out_specs=[_full((r, w)), _full((r, wd)), _full((r, wd)), _full((r, wd)), _full((r, wd)),
                   _full((kw - 1, ns, w)), _full((ns, w))],
        out_shape=[sd((r, w), BF16), sd((r, wd), F32), sd((r, wd), F32), sd((r, wd), F32), sd((r, wd), F32),
                   sd((kw - 1, ns, w), F32), sd((ns, w), F32)],
        compiler_params=_cparams(("arbitrary",)),
        name="odd_proj_sample",
    )(*args)


def _paged_attn_kernel(pt_ref, q_ref, kn_ref, vn_ref, dg_ref, *refs, nh, ts, pps, page, nblk, nbp):
    k_refs = refs[0:pps]
    v_refs = refs[pps:2 * pps]
    o_ref = refs[2 * pps]
    kmean_scr, m_scr, l_scr, o_scr = refs[2 * pps + 1:]
    s_idx = pl.program_id(1)
    nsteps = pl.num_programs(1)
    rows = ts * nh
    wd = q_ref.shape[2]
    ppb = MOBA_BLOCK // page
    bps = pps // ppb

    hm = lax.broadcasted_iota(jnp.int32, (nh, wd), 1) // HD_D == lax.broadcasted_iota(jnp.int32, (nh, wd), 0)
    q4 = q_ref[0]
    qexp = jnp.concatenate([jnp.where(hm, jnp.broadcast_to(q4[t:t + 1, :], (nh, wd)), 0.0) for t in range(ts)], axis=0)
    qexp_b = qexp.astype(BF16)

    @pl.when(s_idx == 0)
    def _():
        kmean_scr[...] = jnp.zeros_like(kmean_scr)

    km_lane = lax.broadcasted_iota(jnp.int32, (wd, nbp), 1)
    for c in range(bps):
        blk = s_idx * bps + c
        kt = jnp.concatenate([k_refs[c * ppb + j][0, 0] for j in range(ppb)], axis=1)
        vt = jnp.concatenate([v_refs[c * ppb + j][0, 0] for j in range(ppb)], axis=1)
        kmean_scr[...] = jnp.where(km_lane == blk, jnp.mean(kt, axis=1, keepdims=True), kmean_scr[...])
        s = _dot(qexp_b, kt.astype(BF16))
        m = jnp.max(s, axis=1, keepdims=True)
        p = jnp.exp(s - m).astype(BF16)
        l = jnp.sum(p.astype(F32), axis=1, keepdims=True)
        o_scr[blk] = _dot_nt(p, vt.astype(BF16))
        m_scr[blk] = jnp.broadcast_to(m, (rows, LANES))
        l_scr[blk] = jnp.broadcast_to(l, (rows, LANES))

    @pl.when(s_idx == nsteps - 1)
    def _():
        gates = jnp.dot(qexp, kmean_scr[...], preferred_element_type=F32,
                        precision=lax.Precision.HIGHEST)
        lane = lax.broadcasted_iota(jnp.int32, gates.shape, 1)
        neg_inf = jnp.float32(-jnp.inf)
        g = jnp.where(lane < nblk, gates, neg_inf)
        sel = jnp.zeros(gates.shape, jnp.bool_)
        for _ in range(MOBA_TOPK):
            mx = jnp.max(g, axis=1, keepdims=True)
            idx = jnp.min(jnp.where(g == mx, lane, jnp.int32(nbp)), axis=1, keepdims=True)
            pick = (lane == idx) & (mx > neg_inf)
            sel = sel | pick
            g = jnp.where(pick, neg_inf, g)

        kn = kn_ref[0]
        vn = vn_ref[0]
        s_own = _dot_nt(qexp_b, kn.astype(BF16))
        tq = lax.broadcasted_iota(jnp.int32, s_own.shape, 0) // nh
        tk = lax.broadcasted_iota(jnp.int32, s_own.shape, 1)
        s_own = jnp.where(tk <= tq, s_own, MASK_NEG)
        m_own = jnp.max(s_own, axis=1, keepdims=True)
        m_tot = m_own
        sel_cols = []
        for n in range(nblk):
            sc = sel[:, n:n + 1]
            sel_cols.append(sc)
            m_tot = jnp.maximum(m_tot, jnp.where(sc, m_scr[n][:, 0:1], M_INIT))
        p_own = jnp.exp(s_own - m_tot)
        num = jnp.zeros((rows, wd), F32)
        for t in range(ts):
            num = num + p_own[:, t:t + 1] * vn[t:t + 1, :]
        den = jnp.sum(p_own, axis=1, keepdims=True)
        for n in range(nblk):
            wgt = jnp.where(sel_cols[n], jnp.exp(m_scr[n][:, 0:1] - m_tot), 0.0)
            num = num + wgt * o_scr[n]
            den = den + wgt * l_scr[n][:, 0:1]
        out = num / den
        att = jnp.concatenate(
            [jnp.sum(jnp.where(hm, out[t * nh:(t + 1) * nh, :], 0.0), axis=0, keepdims=True) for t in range(ts)], axis=0)
        o_ref[0] = (att * dg_ref[0]).astype(BF16)


def _paged_attn(page_table, q, k_new, v_new, dg, cache_k, cache_v, layer):
    ns, ts, wd = q.shape
    npages = page_table.shape[1]
    page = cache_k.shape[3]
    nh = wd // HD_D
    pps = min(PAGES_PER_STEP, npages)
    ppb = MOBA_BLOCK // page
    assert MOBA_BLOCK % page == 0 and npages % pps == 0 and pps % ppb == 0
    nblk = npages // ppb
    nbp = -(-nblk // LANES) * LANES
    kern = functools.partial(_paged_attn_kernel, nh=nh, ts=ts, pps=pps, page=page, nblk=nblk, nbp=nbp)

    def page_spec(p):
        return pl.BlockSpec((1, 1, wd, page), lambda b, s, pt: (pt[b * npages + s * pps + p], layer, 0, 0))

    seq_spec = pl.BlockSpec((1, ts, wd), lambda b, s, pt: (b, 0, 0))
    rows = ts * nh
    grid_spec = pltpu.PrefetchScalarGridSpec(
        num_scalar_prefetch=1,
        grid=(ns, npages // pps),
        in_specs=[seq_spec, seq_spec, seq_spec, seq_spec] + [page_spec(p) for p in range(pps)] * 2,
        out_specs=seq_spec,
        scratch_shapes=[pltpu.VMEM((wd, nbp), F32), pltpu.VMEM((nblk, rows, LANES), F32),
                        pltpu.VMEM((nblk, rows, LANES), F32), pltpu.VMEM((nblk, rows, wd), F32)],
    )
    return pl.pallas_call(
        kern,
        grid_spec=grid_spec,
        out_shape=jax.ShapeDtypeStruct((ns, ts, wd), BF16),
        compiler_params=_cparams(("arbitrary", "arbitrary")),
        name="paged_attn",
    )(page_table.reshape(-1), q, k_new, v_new, dg, *([cache_k] * pps), *([cache_v] * pps))


def _rope_tables(pos):
    half = HD_D // 2
    inv = ROPE_THETA ** (-jnp.arange(half, dtype=F32) / half)
    ang = pos.astype(F32)[:, None] * inv[None, :]
    cos = jnp.cos(ang)
    sin = jnp.sin(ang)
    reps = LANES // HD_D
    return (jnp.concatenate([cos, cos] * reps, axis=1), jnp.concatenate([-sin, sin] * reps, axis=1))


def _block_diag(wb):
    n, d, _ = wb.shape
    eye = jnp.eye(n, dtype=wb.dtype)
    return (eye[:, None, :, None] * wb[:, :, None, :]).reshape(n * d, n * d)


def _row(v):
    return v.reshape(1, -1)


def kernel(x_prompt, x_sample, cache_k, cache_v, page_table, state_conv_a, state_conv_c, state_lru_h, w_in_even,
           conv_a_w, conv_a_b, ln_a_g, ln_a_b, ln_v_g, ln_v_b, gmlp_w, gmlp_b, w_out_even, post_g_even, post_b_even,
           w_in_odd, conv_c_w, conv_c_b, lru_wr, lru_br, lru_wi, lru_bi, lru_lambda, w_out_odd, post_g_odd,
           post_b_odd):
    n_even, n_odd = w_in_even.shape[0], w_in_odd.shape[0]
    depth = n_even + n_odd
    alpha = (2.0 * depth) ** 0.25
    bsz, t_p, d = x_prompt.shape
    ns, ts, _ = x_sample.shape
    n_pool, _, page, nh, hd = cache_k.shape
    assert hd == HD_D
    wd = nh * hd
    past_len = page_table.shape[1] * page
    assert past_len % MOBA_BLOCK == 0 and t_p % MOBA_BLOCK == 0
    w_b = ln_v_g.shape[1]
    gd = w_b // B_GROUPS

    cache_k2 = jnp.transpose(cache_k, (0, 1, 3, 4, 2)).reshape(n_pool, n_odd, wd, page)
    cache_v2 = jnp.transpose(cache_v, (0, 1, 3, 4, 2)).reshape(n_pool, n_odd, wd, page)
    time_major_heads = lambda a: jnp.transpose(a.reshape(bsz, nh, hd, t_p), (0, 3, 1, 2))
    cos_p, sin_p = _rope_tables(jnp.arange(t_p))
    cos_s, sin_s = _rope_tables(past_len + jnp.repeat(jnp.arange(ts), ns))
    nb_p = t_p // MOBA_BLOCK
    nbp_p = -(-nb_p // LANES) * LANES

    xp = x_prompt
    xs = jnp.swapaxes(x_sample, 0, 1).reshape(ts * ns, d)
    tmaj = lambda a: jnp.swapaxes(a, 0, 1)
    outs = {k: [] for k in ("ca_p", "ca_s", "gv_s", "cc_p", "cc_s", "h_p", "h_s", "k_p", "v_p", "k_s", "v_s")}
    tc_s = min(ts, GMLP_CHUNK)
    assert ts % tc_s == 0 and ts == tc_s

    for l in range(depth):
        j = l // 2
        if l % 2 == 0:
            gm_bias_p = jnp.repeat(gmlp_b[j].T, gd, axis=1)
            prm_p = (w_in_even[j].astype(BF16), conv_a_w[j], _row(conv_a_b[j]), _row(ln_a_g[j]), _row(ln_a_b[j]),
                     _row(ln_v_g[j]), _row(ln_v_b[j]), gmlp_w[j], gm_bias_p, w_out_even[j].astype(BF16),
                     _row(post_g_even[j]), _row(post_b_even[j]))
            xp, buf_p = _even_prompt(xp, prm_p, alpha)
            wexp = jnp.repeat(jnp.transpose(gmlp_w[j][:, :ts, :ts], (1, 2, 0)).reshape(ts * ts, B_GROUPS), gd, axis=1)
            bexp = jnp.repeat(gmlp_b[j][:, :ts].T, gd, axis=1)
            prm_s = prm_p[:7] + (wexp, bexp) + prm_p[9:]
            xs, buf_s, gv = _even_sample(xs, tmaj(state_conv_a[:, j]), prm_s, alpha, ns, ts)
            outs["ca_p"].append(buf_p)
            outs["ca_s"].append(tmaj(buf_s))
            outs["gv_s"].append(tmaj(gv.reshape(ts, ns, w_b)))
        else:
            w_gates = jnp.concatenate([_block_diag(lru_wr[j]), _block_diag(lru_wi[j])], axis=1).astype(BF16)
            prm = (w_in_odd[j].astype(BF16), conv_c_w[j], _row(conv_c_b[j]), w_gates, _row(lru_br[j]),
                   _row(lru_bi[j]), _row(lru_lambda[j]))
            w_out = w_out_odd[j].astype(BF16)
            pg, pb = _row(post_g_odd[j]), _row(post_b_odd[j])
            c_out, q, k_t, v_t, ktb, vb, dg, kmean, cb_p, hl_p = _odd_proj_prompt(xp, prm, cos_p, sin_p)
            kmean = jnp.pad(kmean, ((0, 0), (0, nbp_p - nb_p), (0, 0)))
            d_out = _moba_prompt(q, ktb, vb, kmean, dg)
            xp = _out_proj(xp, c_out, d_out, w_out, pg, pb, alpha)
            outs["cc_p"].append(cb_p)
            outs["h_p"].append(hl_p[:, 0])
            outs["k_p"].append(time_major_heads(k_t))
            outs["v_p"].append(time_major_heads(v_t))
            c_out_s, q_s, k_s, v_s, dg_s, cb_s, hl_s = _odd_proj_sample(
                xs, tmaj(state_conv_c[:, j]), state_lru_h[:, j], prm, cos_s, sin_s, ns, ts)
            seq = lambda a: tmaj(a.reshape(ts, ns, wd))
            k_seq, v_seq = seq(k_s), seq(v_s)
            att = _paged_attn(page_table, seq(q_s), k_seq, v_seq, seq(dg_s), cache_k2, cache_v2, j)
            d_out_s = tmaj(att).reshape(ts * ns, wd)
            xs = _out_proj(xs[None], c_out_s[None], d_out_s[None], w_out, pg, pb, alpha)[0]
            outs["cc_s"].append(tmaj(cb_s))
            outs["h_s"].append(hl_s)
            outs["k_s"].append(k_seq.reshape(ns, ts, nh, hd))
            outs["v_s"].append(v_seq.reshape(ns, ts, nh, hd))

    st = lambda name: jnp.stack(outs[name], axis=1)
    return (xp, tmaj(xs.reshape(ts, ns, d)), st("ca_p"), st("ca_s"), st("gv_s"), st("cc_p"), st("cc_s"),
            st("h_p"), st("h_s"), st("k_p"), st("v_p"), st("k_s"), st("v_s"))
```

```python
import functools
import math

import jax
import jax.numpy as jnp
import numpy as np
from jax import lax
from jax.experimental import pallas as pl
from jax.experimental.pallas import tpu as pltpu

F32 = jnp.float32
BF16 = jnp.bfloat16

GMLP_CHUNK = 128
B_GROUPS = 4
C_BLOCKS = 8
LRU_C = 8.0
HD_D = 64
MOBA_BLOCK = 256
MOBA_TOPK = 3
ROPE_THETA = 10000.0
LN_EPS = 1e-5

LANES = 128
SUBLANES = 8
VMEM_LIMIT_BYTES = 56 * 1024 * 1024

MASK_NEG = -1e30
M_INIT = -1e29

PROMPT_TM = 512
CONV_HALO = 32
PAGES_PER_STEP = 8
MOBA_KEY_CHUNK = 1024
MOBA_Q_TILE = 1024


def _ln(x, g, b):
    mu = jnp.mean(x, axis=-1, keepdims=True)
    xc = x - mu
    var = jnp.mean(xc * xc, axis=-1, keepdims=True)
    return xc * lax.rsqrt(var + LN_EPS) * g + b


def _sigmoid(x):
    return jax.nn.sigmoid(x)


def _silu(x):
    return x * jax.nn.sigmoid(x)


def _gelu(x):
    c = math.sqrt(2.0 / math.pi)
    return x * (0.5 * (1.0 + jnp.tanh(c * (x + 0.044715 * (x * x * x)))))


def _softplus(y):
    return jnp.maximum(y, 0.0) + jnp.log1p(jnp.exp(-jnp.abs(y)))


def _dot(a, b):
    return jnp.dot(a, b, preferred_element_type=F32)


def _dot_nt(a, b, precision=None):
    return lax.dot_general(a, b, (((1,), (1,)), ((), ())), preferred_element_type=F32, precision=precision)


def _cparams(sem):
    return pltpu.CompilerParams(dimension_semantics=sem, vmem_limit_bytes=VMEM_LIMIT_BYTES)


def _full(shape):
    n = len(shape)
    return pl.BlockSpec(shape, lambda *_: (0,) * n)


def _causal_conv_tile(scr, a_in, cw_ref, cb_ref, buf_ref, first_tile, last_tile, tm, kw):
    w = a_in.shape[1]
    nl = w // LANES

    @pl.when(first_tile)
    def _():
        scr[:, 0:CONV_HALO, :] = jnp.zeros((nl, CONV_HALO, LANES), F32)

    out = []
    for j in range(nl):
        lanes = slice(j * LANES, (j + 1) * LANES)
        scr[j, CONV_HALO:CONV_HALO + tm, :] = a_in[:, lanes]
        acc = jnp.zeros((tm, LANES), F32) + cb_ref[:, lanes]
        for k in range(kw):
            acc = acc + cw_ref[k:k + 1, lanes] * scr[j, pl.ds(CONV_HALO - (kw - 1) + k, tm), :]
        out.append(acc)

    @pl.when(last_tile)
    def _():
        for j in range(nl):
            buf_ref[0, :, j * LANES:(j + 1) * LANES] = scr[j, pl.ds(CONV_HALO + tm - (kw - 1), kw - 1), :]

    for j in range(nl):
        scr[j, 0:CONV_HALO, :] = scr[j, pl.ds(tm, CONV_HALO), :]
    return jnp.concatenate(out, axis=1)


def _even_prompt_kernel(x_ref, win_ref, cw_ref, cb_ref, lag_ref, lab_ref, lvg_ref, lvb_ref, gmw_ref, gmb_ref,
                        wout_ref, pg_ref, pb_ref, y_ref, buf_ref, scr, *, alpha, tm, w, kw):
    t = pl.program_id(1)
    nt = pl.num_programs(1)
    x = x_ref[0]
    z = _dot(x.astype(BF16), win_ref[...])
    a_val, a_glu, a_gate = z[:, 0:w], z[:, w:2 * w], z[:, 2 * w:3 * w]
    b_u, b_v, b_gate = z[:, 3 * w:4 * w], z[:, 4 * w:5 * w], z[:, 5 * w:6 * w]

    a_in = a_val * _sigmoid(a_glu)
    a_conv = _causal_conv_tile(scr, a_in, cw_ref, cb_ref, buf_ref, t == 0, t == nt - 1, tm, kw)
    a_out = _silu(_ln(a_conv, lag_ref[...], lab_ref[...])) * _silu(a_gate)

    u = _gelu(b_u)
    v = _ln(_gelu(b_v), lvg_ref[...], lvb_ref[...]).astype(BF16)
    gd = w // B_GROUPS
    ri = lax.broadcasted_iota(jnp.int32, (GMLP_CHUNK, GMLP_CHUNK), 0)
    ci = lax.broadcasted_iota(jnp.int32, (GMLP_CHUNK, GMLP_CHUNK), 1)
    wmix = [jnp.where(ri >= ci, gmw_ref[g], 0.0).astype(BF16) for g in range(B_GROUPS)]
    rows = []
    for c in range(tm // GMLP_CHUNK):
        vc = v[c * GMLP_CHUNK:(c + 1) * GMLP_CHUNK, :]
        rows.append(jnp.concatenate(
            [_dot(wmix[g], vc[:, g * gd:(g + 1) * gd]) for g in range(B_GROUPS)], axis=1) + gmb_ref[...])
    mixed = jnp.concatenate(rows, axis=0)
    b_out = u * mixed * _silu(b_gate)

    y = _dot(a_out.astype(BF16), wout_ref[0:w, :]) + _dot(b_out.astype(BF16), wout_ref[w:2 * w, :])
    y_ref[0] = _ln(alpha * x + y, pg_ref[...], pb_ref[...])


def _even_prompt(x, prm, alpha):
    (w_in, conv_w, conv_b, ln_a_g, ln_a_b, ln_v_g, ln_v_b, gm_w, gm_bias, w_out, pg, pb) = prm
    bsz, t, d = x.shape
    tm = min(PROMPT_TM, t)
    w = conv_w.shape[1]
    kw = conv_w.shape[0]
    assert t % tm == 0 and tm % GMLP_CHUNK == 0 and kw - 1 <= CONV_HALO <= tm
    kern = functools.partial(_even_prompt_kernel, alpha=alpha, tm=tm, w=w, kw=kw)
    return pl.pallas_call(
        kern,
        grid=(bsz, t // tm),
        in_specs=[pl.BlockSpec((1, tm, d), lambda b, i: (b, i, 0)),
                  _full(w_in.shape), _full(conv_w.shape), _full(conv_b.shape), _full(ln_a_g.shape),
                  _full(ln_a_b.shape), _full(ln_v_g.shape), _full(ln_v_b.shape), _full(gm_w.shape),
                  _full(gm_bias.shape), _full(w_out.shape), _full(pg.shape), _full(pb.shape)],
        out_specs=[pl.BlockSpec((1, tm, d), lambda b, i: (b, i, 0)),
                   pl.BlockSpec((1, kw - 1, w), lambda b, i: (b, 0, 0))],
        out_shape=[jax.ShapeDtypeStruct((bsz, t, d), F32), jax.ShapeDtypeStruct((bsz, kw - 1, w), F32)],
        scratch_shapes=[pltpu.VMEM((w // LANES, CONV_HALO + tm, LANES), F32)],
        compiler_params=_cparams(("arbitrary", "arbitrary")),
        name="even_prompt",
    )(x, w_in, conv_w, conv_b, ln_a_g, ln_a_b, ln_v_g, ln_v_b, gm_w, gm_bias, w_out, pg, pb)


def _rope(x, cos, sin_signed):
    n = x.shape[1]
    reps = n // LANES
    cos_f = jnp.concatenate([cos] * reps, axis=1)
    sin_f = jnp.concatenate([sin_signed] * reps, axis=1)
    lane = lax.broadcasted_iota(jnp.int32, x.shape, 1)
    first = (lane % HD_D) < (HD_D // 2)
    rot = jnp.where(first, pltpu.roll(x, n - HD_D // 2, 1), pltpu.roll(x, HD_D // 2, 1))
    return x * cos_f + rot * sin_f


def _lru_coeffs(xc, wg_ref, br, bi, lam, w):
    gates = _dot(xc.astype(BF16), wg_ref[...])
    r = _sigmoid(gates[:, 0:w] + br)
    ig = _sigmoid(gates[:, w:2 * w] + bi)
    log_a = (-LRU_C) * r * _softplus(-lam)
    a = jnp.exp(log_a)
    bx = jnp.sqrt(1.0 - a * a) * (ig * xc)
    return a, bx


def _scan_tile(a, b, h0):
    tm = a.shape[0]
    row = lax.broadcasted_iota(jnp.int32, a.shape, 0) % SUBLANES
    s = 1
    while s < SUBLANES:
        a_sh = pltpu.roll(a, s, 0)
        b_sh = pltpu.roll(b, s, 0)
        ok = row >= s
        b = jnp.where(ok, a * b_sh + b, b)
        a = jnp.where(ok, a * a_sh, a)
        s *= 2
    h = h0
    hs = []
    for g in range(tm // SUBLANES):
        rows = slice(g * SUBLANES, (g + 1) * SUBLANES)
        hg = b[rows] + a[rows] * h
        hs.append(hg)
        h = hg[SUBLANES - 1:SUBLANES]
    return jnp.concatenate(hs, axis=0), h


def _odd_proj_prompt_kernel(x_ref, win_ref, cw_ref, cb_ref, wg_ref, br_ref, bi_ref, lam_ref, cos_ref, sin_ref,
                            cout_ref, q_ref, kt_ref, vt_ref, ktb_ref, vb_ref, dg_ref, kmean_ref, buf_ref, hl_ref,
                            scr, h_scr, *, tm, w, kw, scale):
    t = pl.program_id(1)
    nt = pl.num_programs(1)
    x = x_ref[0]
    z = _dot(x.astype(BF16), win_ref[...])
    c_x, c_gate = z[:, 0:w], z[:, w:2 * w]
    wd = (z.shape[1] - 2 * w) // 4
    o = 2 * w
    q, k, v, d_gate = z[:, o:o + wd], z[:, o + wd:o + 2 * wd], z[:, o + 2 * wd:o + 3 * wd], z[:, o + 3 * wd:o + 4 * wd]

    xc = _causal_conv_tile(scr, c_x, cw_ref, cb_ref, buf_ref, t == 0, t == nt - 1, tm, kw)

    a, bx = _lru_coeffs(xc, wg_ref, br_ref[...], bi_ref[...], lam_ref[...], w)

    @pl.when(t == 0)
    def _():
        h_scr[...] = jnp.zeros_like(h_scr)

    h, h_last = _scan_tile(a, bx, h_scr[0:1, :])
    h_scr[0:1, :] = h_last

    @pl.when(t == nt - 1)
    def _():
        hl_ref[0] = h_last

    cout_ref[0] = (h * _silu(c_gate)).astype(BF16)

    cos = cos_ref[...]
    sin = sin_ref[...]
    q_r = _rope(q, cos, sin)
    k_r = _rope(k, cos, sin)
    q_ref[0] = q_r * scale
    k_t = k_r.T
    kt_ref[0] = k_t
    vt_ref[0] = v.T
    vb_ref[0] = v.astype(BF16)
    dg_ref[0] = _silu(d_gate)
    ktb_ref[0, 0] = k_t.astype(BF16)
    nblk = tm // MOBA_BLOCK
    for c in range(nblk):
        kmean_ref[0, pl.ds(t * nblk + c, 1), :] = jnp.mean(
            k_r[c * MOBA_BLOCK:(c + 1) * MOBA_BLOCK, :], axis=0, keepdims=True)


def _odd_proj_prompt(x, prm, cos, sin):
    (w_in, conv_w, conv_b, w_gates, br, bi, lam) = prm
    bsz, t, d = x.shape
    tm = min(PROMPT_TM, t)
    w = conv_w.shape[1]
    kw = conv_w.shape[0]
    wd = (w_in.shape[1] - 2 * w) // 4
    nb = t // MOBA_BLOCK
    kc = min(MOBA_KEY_CHUNK, t)
    tpc = kc // tm
    assert t % tm == 0 and tm % MOBA_BLOCK == 0 and kw - 1 <= CONV_HALO <= tm and kc % tm == 0 and t % kc == 0
    kern = functools.partial(_odd_proj_prompt_kernel, tm=tm, w=w, kw=kw, scale=HD_D ** -0.5)
    row_blk = lambda n: pl.BlockSpec((1, tm, n), lambda b, i: (b, i, 0))
    per_b = lambda r, n: pl.BlockSpec((1, r, n), lambda b, i: (b, 0, 0))
    col_blk = pl.BlockSpec((1, wd, tm), lambda b, i: (b, 0, i))
    return pl.pallas_call(
        kern,
        grid=(bsz, t // tm),
        in_specs=[row_blk(d), _full(w_in.shape), _full(conv_w.shape), _full(conv_b.shape), _full(w_gates.shape),
                  _full(br.shape), _full(bi.shape), _full(lam.shape),
                  pl.BlockSpec((tm, LANES), lambda b, i: (i, 0)), pl.BlockSpec((tm, LANES), lambda b, i: (i, 0))],
        out_specs=[row_blk(w), row_blk(wd), col_blk, col_blk,
                   pl.BlockSpec((1, 1, wd, tm), lambda b, i: (b, i // tpc, 0, i % tpc)),
                   row_blk(wd), row_blk(wd), per_b(nb, wd), per_b(kw - 1, w), per_b(1, w)],
        out_shape=[jax.ShapeDtypeStruct((bsz, t, w), BF16),
                   jax.ShapeDtypeStruct((bsz, t, wd), F32),
                   jax.ShapeDtypeStruct((bsz, wd, t), F32),
                   jax.ShapeDtypeStruct((bsz, wd, t), F32),
                   jax.ShapeDtypeStruct((bsz, t // kc, wd, kc), BF16),
                   jax.ShapeDtypeStruct((bsz, t, wd), BF16),
                   jax.ShapeDtypeStruct((bsz, t, wd), F32),
                   jax.ShapeDtypeStruct((bsz, nb, wd), F32),
                   jax.ShapeDtypeStruct((bsz, kw - 1, w), F32),
                   jax.ShapeDtypeStruct((bsz, 1, w), F32)],
        scratch_shapes=[pltpu.VMEM((w // LANES, CONV_HALO + tm, LANES), F32), pltpu.VMEM((SUBLANES, w), F32)],
        compiler_params=_cparams(("arbitrary", "arbitrary")),
        name="odd_proj_prompt",
    )(x, w_in, conv_w, conv_b, w_gates, br, bi, lam, cos, sin)


def _select_topk_bias(gt, own):
    row = lax.broadcasted_iota(jnp.int32, gt.shape, 0)
    neg_inf = jnp.float32(-jnp.inf)
    g = jnp.where(row < own, gt, neg_inf)
    sel = row == own
    big = jnp.int32(gt.shape[0])
    for _ in range(MOBA_TOPK):
        mx = jnp.max(g, axis=0, keepdims=True)
        idx = jnp.min(jnp.where(g == mx, row, big), axis=0, keepdims=True)
        pick = (row == idx) & (mx > neg_inf)
        sel = sel | pick
        g = jnp.where(pick, neg_inf, g)
    return jnp.where(sel, 0.0, MASK_NEG)


def _moba_prompt_kernel(q_ref, kt_ref, v_ref, oh_ref, kmean_ref, dg_ref, o_ref, acc_ref, m_ref, *, nbp, nbr, kc, bq):
    g = pl.program_id(2)
    own = (g * bq + lax.broadcasted_iota(jnp.int32, (1, bq), 1)) // MOBA_BLOCK
    q = q_ref[0]
    lane = lax.broadcasted_iota(jnp.int32, (bq, LANES), 1)
    head_lanes = [lane < HD_D, lane >= HD_D]
    vlane = lax.broadcasted_iota(jnp.int32, (kc, LANES), 1)
    v_head = [vlane < HD_D, vlane >= HD_D]
    v_one = [(vlane == HD_D).astype(BF16), (vlane == 0).astype(BF16)]
    kml = lax.broadcasted_iota(jnp.int32, (nbr, LANES), 1)
    km_heads = [kml < HD_D, kml >= HD_D]
    kmean = kmean_ref[0, 0:nbr, :]

    qaug = []
    for h in range(2):
        gt = _dot_nt(jnp.where(km_heads[h], kmean, 0.0), q, precision=lax.Precision.HIGHEST)
        bias_t = _select_topk_bias(gt, own)
        if nbp > nbr:
            bias_t = jnp.concatenate([bias_t, jnp.full((nbp - nbr, bq), MASK_NEG, F32)], axis=0)
        qh = jnp.where(head_lanes[h], q, 0.0).astype(BF16)
        qaug.append(jnp.concatenate([qh, bias_t.T.astype(BF16)], axis=1))

    acc_ref[...] = jnp.zeros_like(acc_ref)
    m_ref[...] = jnp.full(m_ref.shape, M_INIT, F32)

    def chunk(c, causal):
        start = pl.multiple_of(c * kc, kc)
        vn = v_ref[0, pl.ds(start, kc), :]
        kaug = jnp.concatenate([kt_ref[0, c], oh_ref[c]], axis=0)
        for h in range(2):
            s = _dot(qaug[h], kaug)
            if causal:
                qpos = lax.broadcasted_iota(jnp.int32, (bq, kc), 0) + g * bq
                kpos = lax.broadcasted_iota(jnp.int32, (bq, kc), 1) + c * kc
                s = jnp.where(kpos <= qpos, s, MASK_NEG)
            m_prev = m_ref[h]
            m_new = jnp.maximum(m_prev, jnp.max(s, axis=1, keepdims=True))
            p = jnp.exp(s - jnp.concatenate([m_new] * (kc // LANES), axis=1))
            alpha = jnp.exp(m_prev - m_new)
            vh = jnp.where(v_head[h], vn, v_one[h])
            acc_ref[h] = alpha * acc_ref[h] + _dot(p.astype(BF16), vh)
            m_ref[h] = m_new

    last = ((g + 1) * bq - 1) // kc

    def body(c, carry):
        chunk(c, False)
        return carry

    lax.fori_loop(0, last, body, 0)
    chunk(last, True)

    acc_a = acc_ref[0]
    acc_b = acc_ref[1]
    out = jnp.where(head_lanes[0], acc_a / acc_a[:, HD_D:HD_D + 1], acc_b / acc_b[:, 0:1])
    o_ref[0] = (out * dg_ref[0]).astype(BF16)


def _moba_prompt(q, ktc, vb, onehot, kmean, dg):
    bsz, t, wd = q.shape
    nq = t // MOBA_BLOCK
    npairs = wd // LANES
    nbp = kmean.shape[1]
    nc, _, kc = ktc.shape[1:]
    nbr = -(-nq // SUBLANES) * SUBLANES
    bq = min(MOBA_Q_TILE, kc)
    assert t % MOBA_BLOCK == 0 and wd % LANES == 0 and nbp % LANES == 0 and nq <= nbp and kc % bq == 0
    assert bq % MOBA_BLOCK == 0
    kern = functools.partial(_moba_prompt_kernel, nbp=nbp, nbr=nbr, kc=kc, bq=bq)
    qblk = pl.BlockSpec((1, bq, LANES), lambda b, p, i: (b, i, p))
    seq = lambda r: pl.BlockSpec((1, r, LANES), lambda b, p, i: (b, 0, p))
    ktblk = pl.BlockSpec((1, nc, LANES, kc), lambda b, p, i: (b, 0, p, 0))
    return pl.pallas_call(
        kern,
        grid=(bsz, npairs, t // bq),
        in_specs=[qblk, ktblk, seq(t), _full(onehot.shape), seq(nbp), qblk],
        out_specs=qblk,
        out_shape=jax.ShapeDtypeStruct((bsz, t, wd), BF16),
        scratch_shapes=[pltpu.VMEM((2, bq, LANES), F32), pltpu.VMEM((2, bq, LANES), F32)],
        compiler_params=_cparams(("arbitrary", "arbitrary", "arbitrary")),
        name="moba_prompt",
    )(q, ktc, vb, onehot, kmean, dg)


def _out_proj_kernel(x_ref, c_ref, d_ref, wout_ref, pg_ref, pb_ref, y_ref, *, alpha, w):
    x = x_ref[0]
    y = _dot(c_ref[0], wout_ref[0:w, :]) + _dot(d_ref[0], wout_ref[w:, :])
    y_ref[0] = _ln(alpha * x + y, pg_ref[...], pb_ref[...])


def _out_proj(x, c_out, d_out, w_out, pg, pb, alpha):
    bsz, t, d = x.shape
    tm = min(PROMPT_TM, t)
    w = c_out.shape[2]
    assert t % tm == 0
    kern = functools.partial(_out_proj_kernel, alpha=alpha, w=w)
    blk = lambda n: pl.BlockSpec((1, tm, n), lambda b, i: (b, i, 0))
    return pl.pallas_call(
        kern,
        grid=(bsz, t // tm),
        in_specs=[blk(d), blk(w), blk(d_out.shape[2]), _full(w_out.shape), _full(pg.shape), _full(pb.shape)],
        out_specs=blk(d),
        out_shape=jax.ShapeDtypeStruct((bsz, t, d), F32),
        compiler_params=_cparams(("arbitrary", "arbitrary")),
        name="out_proj",
    )(x, c_out, d_out, w_out, pg, pb)


def _sample_conv(state_ref, a_in, cw_ref, cb, ns, ts, kw):
    def xp(s):
        return state_ref[s] if s < kw - 1 else a_in[(s - (kw - 1)) * ns:(s - (kw - 2)) * ns, :]

    out = []
    for t in range(ts):
        acc = jnp.zeros((ns, a_in.shape[1]), F32) + cb
        for k in range(kw):
            acc = acc + cw_ref[k:k + 1, :] * xp(t + k)
        out.append(acc)
    return jnp.concatenate(out, axis=0), [xp(ts + s) for s in range(kw - 1)]


def _even_sample_kernel(x_ref, st_ref, win_ref, cw_ref, cb_ref, lag_ref, lab_ref, lvg_ref, lvb_ref, wexp_ref,
                        bexp_ref, wout_ref, pg_ref, pb_ref, y_ref, buf_ref, gv_ref, *, alpha, ns, ts, w, kw):
    x = x_ref[...]
    z = _dot(x.astype(BF16), win_ref[...])
    a_val, a_glu, a_gate = z[:, 0:w], z[:, w:2 * w], z[:, 2 * w:3 * w]
    b_u, b_v, b_gate = z[:, 3 * w:4 * w], z[:, 4 * w:5 * w], z[:, 5 * w:6 * w]
    a_in = a_val * _sigmoid(a_glu)
    a_conv, new_state = _sample_conv(st_ref, a_in, cw_ref, cb_ref[...], ns, ts, kw)
    for s in range(kw - 1):
        buf_ref[s] = new_state[s]
    a_out = _silu(_ln(a_conv, lag_ref[...], lab_ref[...])) * _silu(a_gate)

    u = _gelu(b_u)
    v = _ln(_gelu(b_v), lvg_ref[...], lvb_ref[...])
    gv_ref[...] = v
    mixed = []
    for t in range(ts):
        acc = jnp.zeros((ns, w), F32)
        for s in range(t + 1):
            acc = acc + wexp_ref[t * ts + s:t * ts + s + 1, :] * v[s * ns:(s + 1) * ns, :]
        mixed.append(acc + bexp_ref[t:t + 1, :])
    b_out = u * jnp.concatenate(mixed, axis=0) * _silu(b_gate)
    y = _dot(a_out.astype(BF16), wout_ref[0:w, :]) + _dot(b_out.astype(BF16), wout_ref[w:2 * w, :])
    y_ref[...] = _ln(alpha * x + y, pg_ref[...], pb_ref[...])


def _even_sample(x, state, prm, alpha, ns, ts):
    (w_in, conv_w, conv_b, ln_a_g, ln_a_b, ln_v_g, ln_v_b, wexp, bexp, w_out, pg, pb) = prm
    r, d = x.shape
    w = conv_w.shape[1]
    kw = conv_w.shape[0]
    kern = functools.partial(_even_sample_kernel, alpha=alpha, ns=ns, ts=ts, w=w, kw=kw)
    args = (x, state, w_in, conv_w, conv_b, ln_a_g, ln_a_b, ln_v_g, ln_v_b, wexp, bexp, w_out, pg, pb)
    return pl.pallas_call(
        kern,
        grid=(1,),
        in_specs=[_full(a.shape) for a in args],
        out_specs=[_full((r, d)), _full((kw - 1, ns, w)), _full((r, w))],
        out_shape=[jax.ShapeDtypeStruct((r, d), F32), jax.ShapeDtypeStruct((kw - 1, ns, w), F32),
                   jax.ShapeDtypeStruct((r, w), F32)],
        compiler_params=_cparams(("arbitrary",)),
        name="even_sample",
    )(*args)


def _odd_proj_sample_kernel(x_ref, st_ref, h0_ref, win_ref, cw_ref, cb_ref, wg_ref, br_ref, bi_ref, lam_ref,
                            cos_ref, sin_ref, cout_ref, q_ref, k_ref, v_ref, dg_ref, buf_ref, hl_ref,
                            *, ns, ts, w, kw, scale):
    x = x_ref[...]
    z = _dot(x.astype(BF16), win_ref[...])
    c_x, c_gate = z[:, 0:w], z[:, w:2 * w]
    wd = (z.shape[1] - 2 * w) // 4
    o = 2 * w
    q, k, v, d_gate = z[:, o:o + wd], z[:, o + wd:o + 2 * wd], z[:, o + 2 * wd:o + 3 * wd], z[:, o + 3 * wd:o + 4 * wd]
    xc, new_state = _sample_conv(st_ref, c_x, cw_ref, cb_ref[...], ns, ts, kw)
    for s in range(kw - 1):
        buf_ref[s] = new_state[s]
    a, bx = _lru_coeffs(xc, wg_ref, br_ref[...], bi_ref[...], lam_ref[...], w)
    h = h0_ref[...]
    hs = []
    for t in range(ts):
        h = a[t * ns:(t + 1) * ns, :] * h + bx[t * ns:(t + 1) * ns, :]
        hs.append(h)
    hl_ref[...] = h
    cout_ref[...] = (jnp.concatenate(hs, axis=0) * _silu(c_gate)).astype(BF16)
    cos = cos_ref[...]
    sin = sin_ref[...]
    q_ref[...] = _rope(q, cos, sin) * scale
    k_ref[...] = _rope(k, cos, sin)
    v_ref[...] = v
    dg_ref[...] = _silu(d_gate)


def _odd_proj_sample(x, state, h0, prm, cos, sin, ns, ts):
    (w_in, conv_w, conv_b, w_gates, br, bi, lam) = prm
    r, d = x.shape
    w = conv_w.shape[1]
    kw = conv_w.shape[0]
    wd = (w_in.shape[1] - 2 * w) // 4
    kern = functools.partial(_odd_proj_sample_kernel, ns=ns, ts=ts, w=w, kw=kw, scale=HD_D ** -0.5)
    args = (x, state, h0, w_in, conv_w, conv_b, w_gates, br, bi, lam, cos, sin)
    sd = jax.ShapeDtypeStruct
    return pl.pallas_call(
        kern,
        grid=(1,),
        in_specs=[_full(a.shape) for a in args],
        out_specs=[_full((r, w)), _full((r, wd)), _full((r, wd)), _full((r, wd)), _full((r, wd)),
                   _full((kw - 1, ns, w)), _full((ns, w))],
        out_shape=[sd((r, w), BF16), sd((r, wd), F32), sd((r, wd), F32), sd((r, wd), F32), sd((r, wd), F32),
                   sd((kw - 1, ns, w), F32), sd((ns, w), F32)],
        compiler_params=_cparams(("arbitrary",)),
        name="odd_proj_sample",
    )(*args)


def _paged_attn_kernel(pt_ref, q_ref, kn_ref, vn_ref, dg_ref, *refs, nh, ts, pps, page, nblk, nbp):
    k_refs = refs[0:pps]
    v_refs = refs[pps:2 * pps]
    o_ref = refs[2 * pps]
    gate_scr, m_scr, l_scr, o_scr = refs[2 * pps + 1:]
    s_idx = pl.program_id(1)
    nsteps = pl.num_programs(1)
    rows = ts * nh
    wd = q_ref.shape[2]
    ppb = MOBA_BLOCK // page
    bps = pps // ppb

    hm = lax.broadcasted_iota(jnp.int32, (nh, wd), 1) // HD_D == lax.broadcasted_iota(jnp.int32, (nh, wd), 0)
    q4 = q_ref[0]
    qexp = jnp.concatenate([jnp.where(hm, jnp.broadcast_to(q4[t:t + 1, :], (nh, wd)), 0.0) for t in range(ts)], axis=0)
    qexp_b = qexp.astype(BF16)
    q2 = jnp.concatenate([qexp_b, (qexp - qexp_b.astype(F32)).astype(BF16)], axis=0)

    @pl.when(s_idx == 0)
    def _():
        gate_scr[...] = jnp.zeros_like(gate_scr)
        m_scr[...] = jnp.zeros_like(m_scr)
        l_scr[...] = jnp.zeros_like(l_scr)

    g_lane = lax.broadcasted_iota(jnp.int32, (rows, nbp), 1)
    row_pad = jnp.zeros((LANES - rows, MOBA_BLOCK), F32)
    kt_all = jnp.concatenate([k_refs[j][0, 0] for j in range(pps)], axis=1).astype(BF16)
    s2_all = _dot(q2, kt_all)
    s_all = s2_all[0:rows] + s2_all[rows:2 * rows]
    for c in range(bps):
        blk = s_idx * bps + c
        vt = jnp.concatenate([v_refs[c * ppb + j][0, 0] for j in range(ppb)], axis=1)
        s = s_all[:, c * MOBA_BLOCK:(c + 1) * MOBA_BLOCK]
        m = jnp.max(s, axis=1, keepdims=True)
        p = jnp.exp(s - m).astype(BF16).astype(F32)
        gate_scr[...] = jnp.where(g_lane == blk, jnp.mean(s, axis=1, keepdims=True), gate_scr[...])
        m_scr[...] = jnp.where(g_lane == blk, m, m_scr[...])
        l_scr[...] = jnp.where(g_lane == blk, jnp.sum(p, axis=1, keepdims=True), l_scr[...])
        p_t = jnp.concatenate([p, row_pad], axis=0).T.astype(BF16)
        o_scr[blk] = _dot(vt.astype(BF16), p_t)

    @pl.when(s_idx == nsteps - 1)
    def _():
        gates = gate_scr[...]
        lane = lax.broadcasted_iota(jnp.int32, gates.shape, 1)
        neg_inf = jnp.float32(-jnp.inf)
        g = jnp.where(lane < nblk, gates, neg_inf)
        sel = jnp.zeros(gates.shape, jnp.bool_)
        for _ in range(MOBA_TOPK):
            mx = jnp.max(g, axis=1, keepdims=True)
            idx = jnp.min(jnp.where(g == mx, lane, jnp.int32(nbp)), axis=1, keepdims=True)
            pick = (lane == idx) & (mx > neg_inf)
            sel = sel | pick
            g = jnp.where(pick, neg_inf, g)

        kn = kn_ref[0]
        vn = vn_ref[0]
        s_own = _dot_nt(qexp_b, kn.astype(BF16))
        tq = lax.broadcasted_iota(jnp.int32, s_own.shape, 0) // nh
        tk = lax.broadcasted_iota(jnp.int32, s_own.shape, 1)
        s_own = jnp.where(tk <= tq, s_own, MASK_NEG)
        m_own = jnp.max(s_own, axis=1, keepdims=True)
        m_blk = m_scr[...]
        m_tot = jnp.maximum(m_own, jnp.max(jnp.where(sel, m_blk, M_INIT), axis=1, keepdims=True))
        p_own = jnp.exp(s_own - m_tot)
        num = jnp.zeros((rows, wd), F32)
        for t in range(ts):
            num = num + p_own[:, t:t + 1] * vn[t:t + 1, :]
        wgt = jnp.where(sel, jnp.exp(jnp.where(sel, m_blk, M_INIT) - m_tot), 0.0)
        den = jnp.sum(p_own, axis=1, keepdims=True) + jnp.sum(wgt * l_scr[...], axis=1, keepdims=True)
        wgt_t = jnp.concatenate([wgt, jnp.zeros((LANES - rows, nbp), F32)], axis=0).T
        num_t = jnp.zeros((wd, LANES), F32)
        for n in range(nblk):
            num_t = num_t + o_scr[n] * wgt_t[n:n + 1, :]
        out = (num + num_t.T[0:rows, :]) / den
        att = jnp.concatenate(
            [jnp.sum(jnp.where(hm, out[t * nh:(t + 1) * nh, :], 0.0), axis=0, keepdims=True) for t in range(ts)], axis=0)
        o_ref[0] = (att * dg_ref[0]).astype(BF16)


def _paged_attn(page_table, q, k_new, v_new, dg, cache_k, cache_v, layer):
    ns, ts, wd = q.shape
    npages = page_table.shape[1]
    page = cache_k.shape[3]
    nh = wd // HD_D
    pps = min(PAGES_PER_STEP, npages)
    ppb = MOBA_BLOCK // page
    assert MOBA_BLOCK % page == 0 and npages % pps == 0 and pps % ppb == 0
    nblk = npages // ppb
    nbp = -(-nblk // LANES) * LANES
    kern = functools.partial(_paged_attn_kernel, nh=nh, ts=ts, pps=pps, page=page, nblk=nblk, nbp=nbp)

    def page_spec(p):
        return pl.BlockSpec((1, 1, wd, page), lambda b, s, pt: (pt[b * npages + s * pps + p], layer, 0, 0))

    seq_spec = pl.BlockSpec((1, ts, wd), lambda b, s, pt: (b, 0, 0))
    rows = ts * nh
    grid_spec = pltpu.PrefetchScalarGridSpec(
        num_scalar_prefetch=1,
        grid=(ns, npages // pps),
        in_specs=[seq_spec, seq_spec, seq_spec, seq_spec] + [page_spec(p) for p in range(pps)] * 2,
        out_specs=seq_spec,
        scratch_shapes=[pltpu.VMEM((rows, nbp), F32), pltpu.VMEM((rows, nbp), F32), pltpu.VMEM((rows, nbp), F32),
                        pltpu.VMEM((nblk, wd, LANES), F32)],
    )
    return pl.pallas_call(
        kern,
        grid_spec=grid_spec,
        out_shape=jax.ShapeDtypeStruct((ns, ts, wd), BF16),
        compiler_params=_cparams(("arbitrary", "arbitrary")),
        name="paged_attn",
    )(page_table.reshape(-1), q, k_new, v_new, dg, *([cache_k] * pps), *([cache_v] * pps))


def _rope_tables(pos):
    half = HD_D // 2
    inv = ROPE_THETA ** (-jnp.arange(half, dtype=F32) / half)
    ang = pos.astype(F32)[:, None] * inv[None, :]
    cos = jnp.cos(ang)
    sin = jnp.sin(ang)
    reps = LANES // HD_D
    return (jnp.concatenate([cos, cos] * reps, axis=1), jnp.concatenate([-sin, sin] * reps, axis=1))


def _block_diag(wb):
    n, d, _ = wb.shape
    eye = jnp.eye(n, dtype=wb.dtype)
    return (eye[:, None, :, None] * wb[:, :, None, :]).reshape(n * d, n * d)


def _row(v):
    return v.reshape(1, -1)


def kernel(x_prompt, x_sample, cache_k, cache_v, page_table, state_conv_a, state_conv_c, state_lru_h, w_in_even,
           conv_a_w, conv_a_b, ln_a_g, ln_a_b, ln_v_g, ln_v_b, gmlp_w, gmlp_b, w_out_even, post_g_even, post_b_even,
           w_in_odd, conv_c_w, conv_c_b, lru_wr, lru_br, lru_wi, lru_bi, lru_lambda, w_out_odd, post_g_odd,
           post_b_odd):
    n_even, n_odd = w_in_even.shape[0], w_in_odd.shape[0]
    depth = n_even + n_odd
    alpha = (2.0 * depth) ** 0.25
    bsz, t_p, d = x_prompt.shape
    ns, ts, _ = x_sample.shape
    n_pool, _, page, nh, hd = cache_k.shape
    assert hd == HD_D
    wd = nh * hd
    past_len = page_table.shape[1] * page
    assert past_len % MOBA_BLOCK == 0 and t_p % MOBA_BLOCK == 0
    w_b = ln_v_g.shape[1]
    gd = w_b // B_GROUPS

    cache_k2 = jnp.transpose(cache_k, (0, 1, 3, 4, 2)).reshape(n_pool, n_odd, wd, page)
    cache_v2 = jnp.transpose(cache_v, (0, 1, 3, 4, 2)).reshape(n_pool, n_odd, wd, page)
    time_major_heads = lambda a: jnp.transpose(a.reshape(bsz, nh, hd, t_p), (0, 3, 1, 2))
    cos_p, sin_p = _rope_tables(jnp.arange(t_p))
    cos_s, sin_s = _rope_tables(past_len + jnp.repeat(jnp.arange(ts), ns))
    nb_p = t_p // MOBA_BLOCK
    nbp_p = -(-nb_p // LANES) * LANES
    kc = min(MOBA_KEY_CHUNK, t_p)
    onehot = (jnp.arange(nbp_p)[None, :, None] == (jnp.arange(t_p) // MOBA_BLOCK).reshape(t_p // kc, 1, kc)).astype(BF16)

    xp = x_prompt
    xs = jnp.swapaxes(x_sample, 0, 1).reshape(ts * ns, d)
    tmaj = lambda a: jnp.swapaxes(a, 0, 1)
    outs = {k: [] for k in ("ca_p", "ca_s", "gv_s", "cc_p", "cc_s", "h_p", "h_s", "k_p", "v_p", "k_s", "v_s")}
    tc_s = min(ts, GMLP_CHUNK)
    assert ts % tc_s == 0 and ts == tc_s

    for l in range(depth):
        j = l // 2
        if l % 2 == 0:
            gm_bias_p = jnp.repeat(gmlp_b[j].T, gd, axis=1)
            prm_p = (w_in_even[j].astype(BF16), conv_a_w[j], _row(conv_a_b[j]), _row(ln_a_g[j]), _row(ln_a_b[j]),
                     _row(ln_v_g[j]), _row(ln_v_b[j]), gmlp_w[j], gm_bias_p, w_out_even[j].astype(BF16),
                     _row(post_g_even[j]), _row(post_b_even[j]))
            xp, buf_p = _even_prompt(xp, prm_p, alpha)
            wexp = jnp.repeat(jnp.transpose(gmlp_w[j][:, :ts, :ts], (1, 2, 0)).reshape(ts * ts, B_GROUPS), gd, axis=1)
            bexp = jnp.repeat(gmlp_b[j][:, :ts].T, gd, axis=1)
            prm_s = prm_p[:7] + (wexp, bexp) + prm_p[9:]
            xs, buf_s, gv = _even_sample(xs, tmaj(state_conv_a[:, j]), prm_s, alpha, ns, ts)
            outs["ca_p"].append(buf_p)
            outs["ca_s"].append(tmaj(buf_s))
            outs["gv_s"].append(tmaj(gv.reshape(ts, ns, w_b)))
        else:
            w_gates = jnp.concatenate([_block_diag(lru_wr[j]), _block_diag(lru_wi[j])], axis=1).astype(BF16)
            prm = (w_in_odd[j].astype(BF16), conv_c_w[j], _row(conv_c_b[j]), w_gates, _row(lru_br[j]),
                   _row(lru_bi[j]), _row(lru_lambda[j]))
            w_out = w_out_odd[j].astype(BF16)
            pg, pb = _row(post_g_odd[j]), _row(post_b_odd[j])
            c_out, q, k_t, v_t, ktb, vb, dg, kmean, cb_p, hl_p = _odd_proj_prompt(xp, prm, cos_p, sin_p)
            kmean = jnp.pad(kmean, ((0, 0), (0, nbp_p - nb_p), (0, 0)))
            d_out = _moba_prompt(q, ktb, vb, onehot, kmean, dg)
            xp = _out_proj(xp, c_out, d_out, w_out, pg, pb, alpha)
            outs["cc_p"].append(cb_p)
            outs["h_p"].append(hl_p[:, 0])
            outs["k_p"].append(time_major_heads(k_t))
            outs["v_p"].append(time_major_heads(v_t))
            c_out_s, q_s, k_s, v_s, dg_s, cb_s, hl_s = _odd_proj_sample(
                xs, tmaj(state_conv_c[:, j]), state_lru_h[:, j], prm, cos_s, sin_s, ns, ts)
            seq = lambda a: tmaj(a.reshape(ts, ns, wd))
            k_seq, v_seq = seq(k_s), seq(v_s)
            att = _paged_attn(page_table, seq(q_s), k_seq, v_seq, seq(dg_s), cache_k2, cache_v2, j)
            d_out_s = tmaj(att).reshape(ts * ns, wd)
            xs = _out_proj(xs[None], c_out_s[None], d_out_s[None], w_out, pg, pb, alpha)[0]
            outs["cc_s"].append(tmaj(cb_s))
            outs["h_s"].append(hl_s)
            outs["k_s"].append(k_seq.reshape(ns, ts, nh, hd))
            outs["v_s"].append(v_seq.reshape(ns, ts, nh, hd))

    st = lambda name: jnp.stack(outs[name], axis=1)
    return (xp, tmaj(xs.reshape(ts, ns, d)), st("ca_p"), st("ca_s"), st("gv_s"), st("cc_p"), st("cc_s"),
            st("h_p"), st("h_s"), st("k_p"), st("v_p"), st("k_s"), st("v_s"))
```

```python
import functools
import math

import jax
import jax.numpy as jnp
import numpy as np
from jax import lax
from jax.experimental import pallas as pl
from jax.experimental.pallas import tpu as pltpu

F32 = jnp.float32
BF16 = jnp.bfloat16

GMLP_CHUNK = 128
B_GROUPS = 4
C_BLOCKS = 8
LRU_C = 8.0
HD_D = 64
MOBA_BLOCK = 256
MOBA_TOPK = 3
ROPE_THETA = 10000.0
LN_EPS = 1e-5

LANES = 128
SUBLANES = 8
VMEM_LIMIT_BYTES = 56 * 1024 * 1024

MASK_NEG = -1e30
M_INIT = -1e29

PROMPT_TM = 512
CONV_HALO = 32
PAGES_PER_STEP = 16
MOBA_KEY_CHUNK = 1024
MOBA_Q_TILE = 1024


def _ln(x, g, b):
    mu = jnp.mean(x, axis=-1, keepdims=True)
    xc = x - mu
    var = jnp.mean(xc * xc, axis=-1, keepdims=True)
    return xc * lax.rsqrt(var + LN_EPS) * g + b


def _sigmoid(x):
    return jax.nn.sigmoid(x)


def _silu(x):
    return x * jax.nn.sigmoid(x)


def _gelu(x):
    c = math.sqrt(2.0 / math.pi)
    return x * (0.5 * (1.0 + jnp.tanh(c * (x + 0.044715 * (x * x * x)))))


def _softplus(y):
    return jnp.maximum(y, 0.0) + jnp.log1p(jnp.exp(-jnp.abs(y)))


def _dot(a, b):
    return jnp.dot(a, b, preferred_element_type=F32)


def _dot_nt(a, b, precision=None):
    return lax.dot_general(a, b, (((1,), (1,)), ((), ())), preferred_element_type=F32, precision=precision)


def _cparams(sem):
    return pltpu.CompilerParams(dimension_semantics=sem, vmem_limit_bytes=VMEM_LIMIT_BYTES)


def _full(shape):
    n = len(shape)
    return pl.BlockSpec(shape, lambda *_: (0,) * n)


def _zero_conv_halo(scr):
    scr[:, 0:CONV_HALO, :] = jnp.zeros((scr.shape[0], CONV_HALO, LANES), F32)


def _causal_conv_tile(scr, a_in, cw_ref, cb_ref, buf_ref, tm, kw, after_slab=None):
    w = a_in.shape[1]
    nl = w // LANES
    out = []
    for j in range(nl):
        lanes = slice(j * LANES, (j + 1) * LANES)
        scr[j, CONV_HALO:CONV_HALO + tm, :] = a_in[:, lanes]
        acc = jnp.zeros((tm, LANES), F32) + cb_ref[:, lanes]
        for k in range(kw):
            acc = acc + cw_ref[k:k + 1, lanes] * scr[j, pl.ds(CONV_HALO - (kw - 1) + k, tm), :]
        out.append(acc)
        buf_ref[0, :, lanes] = scr[j, pl.ds(CONV_HALO + tm - (kw - 1), kw - 1), :]
        scr[j, 0:CONV_HALO, :] = scr[j, pl.ds(tm, CONV_HALO), :]
        if after_slab is not None:
            after_slab(j)
    return jnp.concatenate(out, axis=1)


def _even_prompt_kernel(x_ref, win_ref, cw_ref, cb_ref, lag_ref, lab_ref, lvg_ref, lvb_ref, gmw_ref, gmb_ref,
                        wout_ref, pg_ref, pb_ref, y_ref, buf_ref, scr, *, alpha, tm, w, kw):
    @pl.when(pl.program_id(1) == 0)
    def _():
        _zero_conv_halo(scr)

    x = x_ref[0]
    z = _dot(x.astype(BF16), win_ref[...])
    a_val, a_glu, a_gate = z[:, 0:w], z[:, w:2 * w], z[:, 2 * w:3 * w]
    b_u, b_v, b_gate = z[:, 3 * w:4 * w], z[:, 4 * w:5 * w], z[:, 5 * w:6 * w]

    a_in = a_val * _sigmoid(a_glu)
    a_conv = _causal_conv_tile(scr, a_in, cw_ref, cb_ref, buf_ref, tm, kw)
    a_out = _silu(_ln(a_conv, lag_ref[...], lab_ref[...])) * _silu(a_gate)

    u = _gelu(b_u)
    v = _ln(_gelu(b_v), lvg_ref[...], lvb_ref[...]).astype(BF16)
    gd = w // B_GROUPS
    ri = lax.broadcasted_iota(jnp.int32, (GMLP_CHUNK, GMLP_CHUNK), 0)
    ci = lax.broadcasted_iota(jnp.int32, (GMLP_CHUNK, GMLP_CHUNK), 1)
    wmix = [jnp.where(ri >= ci, gmw_ref[g], 0.0).astype(BF16) for g in range(B_GROUPS)]
    rows = []
    for c in range(tm // GMLP_CHUNK):
        vc = v[c * GMLP_CHUNK:(c + 1) * GMLP_CHUNK, :]
        rows.append(jnp.concatenate(
            [_dot(wmix[g], vc[:, g * gd:(g + 1) * gd]) for g in range(B_GROUPS)], axis=1) + gmb_ref[...])
    mixed = jnp.concatenate(rows, axis=0)
    b_out = u * mixed * _silu(b_gate)

    y = _dot(a_out.astype(BF16), wout_ref[0:w, :]) + _dot(b_out.astype(BF16), wout_ref[w:2 * w, :])
    y_ref[0] = _ln(alpha * x + y, pg_ref[...], pb_ref[...])


def _even_prompt(x, prm, alpha):
    (w_in, conv_w, conv_b, ln_a_g, ln_a_b, ln_v_g, ln_v_b, gm_w, gm_bias, w_out, pg, pb) = prm
    bsz, t, d = x.shape
    tm = min(PROMPT_TM, t)
    w = conv_w.shape[1]
    kw = conv_w.shape[0]
    assert t % tm == 0 and tm % GMLP_CHUNK == 0 and kw - 1 <= CONV_HALO <= tm
    kern = functools.partial(_even_prompt_kernel, alpha=alpha, tm=tm, w=w, kw=kw)
    return pl.pallas_call(
        kern,
        grid=(bsz, t // tm),
        in_specs=[pl.BlockSpec((1, tm, d), lambda b, i: (b, i, 0)),
                  _full(w_in.shape), _full(conv_w.shape), _full(conv_b.shape), _full(ln_a_g.shape),
                  _full(ln_a_b.shape), _full(ln_v_g.shape), _full(ln_v_b.shape), _full(gm_w.shape),
                  _full(gm_bias.shape), _full(w_out.shape), _full(pg.shape), _full(pb.shape)],
        out_specs=[pl.BlockSpec((1, tm, d), lambda b, i: (b, i, 0)),
                   pl.BlockSpec((1, kw - 1, w), lambda b, i: (b, 0, 0))],
        out_shape=[jax.ShapeDtypeStruct((bsz, t, d), F32), jax.ShapeDtypeStruct((bsz, kw - 1, w), F32)],
        scratch_shapes=[pltpu.VMEM((w // LANES, CONV_HALO + tm, LANES), F32)],
        compiler_params=_cparams(("arbitrary", "arbitrary")),
        name="even_prompt",
    )(x, w_in, conv_w, conv_b, ln_a_g, ln_a_b, ln_v_g, ln_v_b, gm_w, gm_bias, w_out, pg, pb)


def _rope(x, cos, sin_signed):
    n = x.shape[1]
    reps = n // LANES
    cos_f = jnp.concatenate([cos] * reps, axis=1)
    sin_f = jnp.concatenate([sin_signed] * reps, axis=1)
    lane = lax.broadcasted_iota(jnp.int32, x.shape, 1)
    first = (lane % HD_D) < (HD_D // 2)
    rot = jnp.where(first, pltpu.roll(x, n - HD_D // 2, 1), pltpu.roll(x, HD_D // 2, 1))
    return x * cos_f + rot * sin_f


def _lru_coeffs(xc, wg_ref, br, bi, lam, w):
    gates = _dot(xc.astype(BF16), wg_ref[...])
    r = _sigmoid(gates[:, 0:w] + br)
    ig = _sigmoid(gates[:, w:2 * w] + bi)
    log_a = (-LRU_C) * r * _softplus(-lam)
    a = jnp.exp(log_a)
    bx = jnp.sqrt(1.0 - a * a) * (ig * xc)
    return a, bx


def _scan_tile(a, b, h0):
    tm = a.shape[0]
    row = lax.broadcasted_iota(jnp.int32, a.shape, 0) % SUBLANES
    s = 1
    while s < SUBLANES:
        a_sh = pltpu.roll(a, s, 0)
        b_sh = pltpu.roll(b, s, 0)
        ok = row >= s
        b = jnp.where(ok, a * b_sh + b, b)
        a = jnp.where(ok, a * a_sh, a)
        s *= 2
    h = h0
    hs = []
    for g in range(tm // SUBLANES):
        rows = slice(g * SUBLANES, (g + 1) * SUBLANES)
        hg = b[rows] + a[rows] * h
        hs.append(hg)
        h = hg[SUBLANES - 1:SUBLANES]
    return jnp.concatenate(hs, axis=0), h


def _odd_proj_prompt_kernel(x_ref, win_ref, cw_ref, cb_ref, wg_ref, br_ref, bi_ref, lam_ref, cos_ref, sin_ref,
                            cout_ref, q_ref, kt_ref, vt_ref, ktb_ref, vb_ref, dg_ref, kmean_ref, buf_ref, hl_ref,
                            scr, h_scr, *, tm, w, kw, scale):
    t = pl.program_id(1)

    @pl.when(t == 0)
    def _():
        _zero_conv_halo(scr)
        h_scr[...] = jnp.zeros_like(h_scr)

    x = x_ref[0]
    z = _dot(x.astype(BF16), win_ref[...])
    c_x, c_gate = z[:, 0:w], z[:, w:2 * w]
    wd = (z.shape[1] - 2 * w) // 4
    o = 2 * w
    q, k, v, d_gate = z[:, o:o + wd], z[:, o + wd:o + 2 * wd], z[:, o + 2 * wd:o + 3 * wd], z[:, o + 3 * wd:o + 4 * wd]

    xc = _causal_conv_tile(scr, c_x, cw_ref, cb_ref, buf_ref, tm, kw)
    a, bx = _lru_coeffs(xc, wg_ref, br_ref[...], bi_ref[...], lam_ref[...], w)
    h, h_last = _scan_tile(a, bx, h_scr[0:1, :])
    h_scr[0:1, :] = h_last
    hl_ref[0] = h_last
    cout_ref[0] = (h * _silu(c_gate)).astype(BF16)

    cos = cos_ref[...]
    sin = sin_ref[...]
    q_r = _rope(q, cos, sin)
    k_r = _rope(k, cos, sin)
    q_ref[0] = q_r * scale
    k_t = k_r.T
    kt_ref[0] = k_t
    vt_ref[0] = v.T
    vb_ref[0] = v.astype(BF16)
    dg_ref[0] = _silu(d_gate)
    ktb_ref[0, 0] = k_t.astype(BF16)
    nblk = tm // MOBA_BLOCK
    for c in range(nblk):
        kmean_ref[0, pl.ds(t * nblk + c, 1), :] = jnp.mean(
            k_r[c * MOBA_BLOCK:(c + 1) * MOBA_BLOCK, :], axis=0, keepdims=True)


def _odd_proj_prompt(x, prm, cos, sin):
    (w_in, conv_w, conv_b, w_gates, br, bi, lam) = prm
    bsz, t, d = x.shape
    tm = min(PROMPT_TM, t)
    w = conv_w.shape[1]
    kw = conv_w.shape[0]
    wd = (w_in.shape[1] - 2 * w) // 4
    nb = t // MOBA_BLOCK
    kc = min(MOBA_KEY_CHUNK, t)
    tpc = kc // tm
    assert t % tm == 0 and tm % MOBA_BLOCK == 0 and kw - 1 <= CONV_HALO <= tm and kc % tm == 0 and t % kc == 0
    kern = functools.partial(_odd_proj_prompt_kernel, tm=tm, w=w, kw=kw, scale=HD_D ** -0.5)
    row_blk = lambda n: pl.BlockSpec((1, tm, n), lambda b, i: (b, i, 0))
    per_b = lambda r, n: pl.BlockSpec((1, r, n), lambda b, i: (b, 0, 0))
    col_blk = pl.BlockSpec((1, wd, tm), lambda b, i: (b, 0, i))
    return pl.pallas_call(
        kern,
        grid=(bsz, t // tm),
        in_specs=[row_blk(d), _full(w_in.shape), _full(conv_w.shape), _full(conv_b.shape), _full(w_gates.shape),
                  _full(br.shape), _full(bi.shape), _full(lam.shape),
                  pl.BlockSpec((tm, LANES), lambda b, i: (i, 0)), pl.BlockSpec((tm, LANES), lambda b, i: (i, 0))],
        out_specs=[row_blk(w), row_blk(wd), col_blk, col_blk,
                   pl.BlockSpec((1, 1, wd, tm), lambda b, i: (b, i // tpc, 0, i % tpc)),
                   row_blk(wd), row_blk(wd), per_b(nb, wd), per_b(kw - 1, w), per_b(1, w)],
        out_shape=[jax.ShapeDtypeStruct((bsz, t, w), BF16),
                   jax.ShapeDtypeStruct((bsz, t, wd), F32),
                   jax.ShapeDtypeStruct((bsz, wd, t), F32),
                   jax.ShapeDtypeStruct((bsz, wd, t), F32),
                   jax.ShapeDtypeStruct((bsz, t // kc, wd, kc), BF16),
                   jax.ShapeDtypeStruct((bsz, t, wd), BF16),
                   jax.ShapeDtypeStruct((bsz, t, wd), F32),
                   jax.ShapeDtypeStruct((bsz, nb, wd), F32),
                   jax.ShapeDtypeStruct((bsz, kw - 1, w), F32),
                   jax.ShapeDtypeStruct((bsz, 1, w), F32)],
        scratch_shapes=[pltpu.VMEM((w // LANES, CONV_HALO + tm, LANES), F32), pltpu.VMEM((SUBLANES, w), F32)],
        compiler_params=_cparams(("arbitrary", "arbitrary")),
        name="odd_proj_prompt",
    )(x, w_in, conv_w, conv_b, w_gates, br, bi, lam, cos, sin)


def _select_topk_bias(gt, own):
    row = lax.broadcasted_iota(jnp.int32, gt.shape, 0)
    neg_inf = jnp.float32(-jnp.inf)
    g = jnp.where(row < own, gt, neg_inf)
    sel = row == own
    big = jnp.int32(gt.shape[0])
    for _ in range(MOBA_TOPK):
        mx = jnp.max(g, axis=0, keepdims=True)
        idx = jnp.min(jnp.where(g == mx, row, big), axis=0, keepdims=True)
        pick = (row == idx) & (mx > neg_inf)
        sel = sel | pick
        g = jnp.where(pick, neg_inf, g)
    return jnp.where(sel, 0.0, MASK_NEG)


def _moba_prompt_kernel(q_ref, kt_ref, v_ref, oh_ref, kmean_ref, dg_ref, o_ref, acc_ref, m_ref, *, nbp, nbr, kc, bq):
    g = pl.program_id(2)
    own = (g * bq + lax.broadcasted_iota(jnp.int32, (1, bq), 1)) // MOBA_BLOCK
    q = q_ref[0]
    lane = lax.broadcasted_iota(jnp.int32, (bq, LANES), 1)
    head_lanes = [lane < HD_D, lane >= HD_D]
    vlane = lax.broadcasted_iota(jnp.int32, (kc, LANES), 1)
    v_head = [vlane < HD_D, vlane >= HD_D]
    v_one = [(vlane == HD_D).astype(BF16), (vlane == 0).astype(BF16)]
    kml = lax.broadcasted_iota(jnp.int32, (nbr, LANES), 1)
    km_heads = [kml < HD_D, kml >= HD_D]
    kmean = kmean_ref[0, 0:nbr, :]

    qaug = []
    for h in range(2):
        gt = _dot_nt(jnp.where(km_heads[h], kmean, 0.0), q, precision=lax.Precision.HIGHEST)
        bias_t = _select_topk_bias(gt, own)
        if nbp > nbr:
            bias_t = jnp.concatenate([bias_t, jnp.full((nbp - nbr, bq), MASK_NEG, F32)], axis=0)
        qh = jnp.where(head_lanes[h], q, 0.0).astype(BF16)
        qaug.append(jnp.concatenate([qh, bias_t.T.astype(BF16)], axis=1))

    acc_ref[...] = jnp.zeros_like(acc_ref)
    m_ref[...] = jnp.full(m_ref.shape, M_INIT, F32)

    def chunk(c, causal):
        start = pl.multiple_of(c * kc, kc)
        vn = v_ref[0, pl.ds(start, kc), :]
        kaug = jnp.concatenate([kt_ref[0, c], oh_ref[c]], axis=0)
        for h in range(2):
            s = _dot(qaug[h], kaug)
            if causal:
                qpos = lax.broadcasted_iota(jnp.int32, (bq, kc), 0) + g * bq
                kpos = lax.broadcasted_iota(jnp.int32, (bq, kc), 1) + c * kc
                s = jnp.where(kpos <= qpos, s, MASK_NEG)
            m_prev = m_ref[h]
            m_new = jnp.maximum(m_prev, jnp.max(s, axis=1, keepdims=True))
            p = jnp.exp(s - jnp.concatenate([m_new] * (kc // LANES), axis=1))
            alpha = jnp.exp(m_prev - m_new)
            vh = jnp.where(v_head[h], vn, v_one[h])
            acc_ref[h] = alpha * acc_ref[h] + _dot(p.astype(BF16), vh)
            m_ref[h] = m_new

    last = ((g + 1) * bq - 1) // kc

    def body(c, carry):
        chunk(c, False)
        return carry

    lax.fori_loop(0, last, body, 0)
    chunk(last, True)

    acc_a = acc_ref[0]
    acc_b = acc_ref[1]
    out = jnp.where(head_lanes[0], acc_a / acc_a[:, HD_D:HD_D + 1], acc_b / acc_b[:, 0:1])
    o_ref[0] = (out * dg_ref[0]).astype(BF16)


def _moba_prompt(q, ktc, vb, onehot, kmean, dg):
    bsz, t, wd = q.shape
    nq = t // MOBA_BLOCK
    npairs = wd // LANES
    nbp = kmean.shape[1]
    nc, _, kc = ktc.shape[1:]
    nbr = -(-nq // SUBLANES) * SUBLANES
    bq = min(MOBA_Q_TILE, kc)
    assert t % MOBA_BLOCK == 0 and wd % LANES == 0 and nbp % LANES == 0 and nq <= nbp and kc % bq == 0
    assert bq % MOBA_BLOCK == 0
    kern = functools.partial(_moba_prompt_kernel, nbp=nbp, nbr=nbr, kc=kc, bq=bq)
    qblk = pl.BlockSpec((1, bq, LANES), lambda b, p, i: (b, i, p))
    seq = lambda r: pl.BlockSpec((1, r, LANES), lambda b, p, i: (b, 0, p))
    ktblk = pl.BlockSpec((1, nc, LANES, kc), lambda b, p, i: (b, 0, p, 0))
    return pl.pallas_call(
        kern,
        grid=(bsz, npairs, t // bq),
        in_specs=[qblk, ktblk, seq(t), _full(onehot.shape), seq(nbp), qblk],
        out_specs=qblk,
        out_shape=jax.ShapeDtypeStruct((bsz, t, wd), BF16),
        scratch_shapes=[pltpu.VMEM((2, bq, LANES), F32), pltpu.VMEM((2, bq, LANES), F32)],
        compiler_params=_cparams(("arbitrary", "arbitrary", "arbitrary")),
        name="moba_prompt",
    )(q, ktc, vb, onehot, kmean, dg)


def _out_proj_kernel(x_ref, c_ref, d_ref, wout_ref, pg_ref, pb_ref, y_ref, *, alpha, w):
    x = x_ref[0]
    y = _dot(c_ref[0], wout_ref[0:w, :]) + _dot(d_ref[0], wout_ref[w:, :])
    y_ref[0] = _ln(alpha * x + y, pg_ref[...], pb_ref[...])


def _out_proj(x, c_out, d_out, w_out, pg, pb, alpha):
    bsz, t, d = x.shape
    tm = min(PROMPT_TM, t)
    w = c_out.shape[2]
    assert t % tm == 0
    kern = functools.partial(_out_proj_kernel, alpha=alpha, w=w)
    blk = lambda n: pl.BlockSpec((1, tm, n), lambda b, i: (b, i, 0))
    return pl.pallas_call(
        kern,
        grid=(bsz, t // tm),
        in_specs=[blk(d), blk(w), blk(d_out.shape[2]), _full(w_out.shape), _full(pg.shape), _full(pb.shape)],
        out_specs=blk(d),
        out_shape=jax.ShapeDtypeStruct((bsz, t, d), F32),
        compiler_params=_cparams(("arbitrary", "arbitrary")),
        name="out_proj",
    )(x, c_out, d_out, w_out, pg, pb)


def _sample_conv(state_ref, a_in, cw_ref, cb, ns, ts, kw):
    def xp(s):
        return state_ref[s] if s < kw - 1 else a_in[(s - (kw - 1)) * ns:(s - (kw - 2)) * ns, :]

    out = []
    for t in range(ts):
        acc = jnp.zeros((ns, a_in.shape[1]), F32) + cb
        for k in range(kw):
            acc = acc + cw_ref[k:k + 1, :] * xp(t + k)
        out.append(acc)
    return jnp.concatenate(out, axis=0), [xp(ts + s) for s in range(kw - 1)]


def _even_sample_kernel(x_ref, st_ref, win_ref, cw_ref, cb_ref, lag_ref, lab_ref, lvg_ref, lvb_ref, wexp_ref,
                        bexp_ref, wout_ref, pg_ref, pb_ref, y_ref, buf_ref, gv_ref, *, alpha, ns, ts, w, kw):
    x = x_ref[...]
    z = _dot(x.astype(BF16), win_ref[...])
    a_val, a_glu, a_gate = z[:, 0:w], z[:, w:2 * w], z[:, 2 * w:3 * w]
    b_u, b_v, b_gate = z[:, 3 * w:4 * w], z[:, 4 * w:5 * w], z[:, 5 * w:6 * w]
    a_in = a_val * _sigmoid(a_glu)
    a_conv, new_state = _sample_conv(st_ref, a_in, cw_ref, cb_ref[...], ns, ts, kw)
    for s in range(kw - 1):
        buf_ref[s] = new_state[s]
    a_out = _silu(_ln(a_conv, lag_ref[...], lab_ref[...])) * _silu(a_gate)

    u = _gelu(b_u)
    v = _ln(_gelu(b_v), lvg_ref[...], lvb_ref[...])
    gv_ref[...] = v
    mixed = []
    for t in range(ts):
        acc = jnp.zeros((ns, w), F32)
        for s in range(t + 1):
            acc = acc + wexp_ref[t * ts + s:t * ts + s + 1, :] * v[s * ns:(s + 1) * ns, :]
        mixed.append(acc + bexp_ref[t:t + 1, :])
    b_out = u * jnp.concatenate(mixed, axis=0) * _silu(b_gate)
    y = _dot(a_out.astype(BF16), wout_ref[0:w, :]) + _dot(b_out.astype(BF16), wout_ref[w:2 * w, :])
    y_ref[...] = _ln(alpha * x + y, pg_ref[...], pb_ref[...])


def _even_sample(x, state, prm, alpha, ns, ts):
    (w_in, conv_w, conv_b, ln_a_g, ln_a_b, ln_v_g, ln_v_b, wexp, bexp, w_out, pg, pb) = prm
    r, d = x.shape
    w = conv_w.shape[1]
    kw = conv_w.shape[0]
    kern = functools.partial(_even_sample_kernel, alpha=alpha, ns=ns, ts=ts, w=w, kw=kw)
    args = (x, state, w_in, conv_w, conv_b, ln_a_g, ln_a_b, ln_v_g, ln_v_b, wexp, bexp, w_out, pg, pb)
    return pl.pallas_call(
        kern,
        grid=(1,),
        in_specs=[_full(a.shape) for a in args],
        out_specs=[_full((r, d)), _full((kw - 1, ns, w)), _full((r, w))],
        out_shape=[jax.ShapeDtypeStruct((r, d), F32), jax.ShapeDtypeStruct((kw - 1, ns, w), F32),
                   jax.ShapeDtypeStruct((r, w), F32)],
        compiler_params=_cparams(("arbitrary",)),
        name="even_sample",
    )(*args)


def _odd_proj_sample_kernel(x_ref, st_ref, h0_ref, win_ref, cw_ref, cb_ref, wg_ref, br_ref, bi_ref, lam_ref,
                            cos_ref, sin_ref, cout_ref, q_ref, k_ref, v_ref, dg_ref, buf_ref, hl_ref,
                            *, ns, ts, w, kw, scale):
    x = x_ref[...]
    z = _dot(x.astype(BF16), win_ref[...])
    c_x, c_gate = z[:, 0:w], z[:, w:2 * w]
    wd = (z.shape[1] - 2 * w) // 4
    o = 2 * w
    q, k, v, d_gate = z[:, o:o + wd], z[:, o + wd:o + 2 * wd], z[:, o + 2 * wd:o + 3 * wd], z[:, o + 3 * wd:o + 4 * wd]
    xc, new_state = _sample_conv(st_ref, c_x, cw_ref, cb_ref[...], ns, ts, kw)
    for s in range(kw - 1):
        buf_ref[s] = new_state[s]
    a, bx = _lru_coeffs(xc, wg_ref, br_ref[...], bi_ref[...], lam_ref[...], w)
    h = h0_ref[...]
    hs = []
    for t in range(ts):
        h = a[t * ns:(t + 1) * ns, :] * h + bx[t * ns:(t + 1) * ns, :]
        hs.append(h)
    hl_ref[...] = h
    cout_ref[...] = (jnp.concatenate(hs, axis=0) * _silu(c_gate)).astype(BF16)
    cos = cos_ref[...]
    sin = sin_ref[...]
    q_ref[...] = _rope(q, cos, sin) * scale
    k_ref[...] = _rope(k, cos, sin)
    v_ref[...] = v
    dg_ref[...] = _silu(d_gate)


def _odd_proj_sample(x, state, h0, prm, cos, sin, ns, ts):
    (w_in, conv_w, conv_b, w_gates, br, bi, lam) = prm
    r, d = x.shape
    w = conv_w.shape[1]
    kw = conv_w.shape[0]
    wd = (w_in.shape[1] - 2 * w) // 4
    kern = functools.partial(_odd_proj_sample_kernel, ns=ns, ts=ts, w=w, kw=kw, scale=HD_D ** -0.5)
    args = (x, state, h0, w_in, conv_w, conv_b, w_gates, br, bi, lam, cos, sin)
    sd = jax.ShapeDtypeStruct
    return pl.pallas_call(
        kern,
        grid=(1,),
        in_specs=[_full(a.shape) for a in args],
        out_specs=[_full((r, w)), _full((r, wd)), _full((r, wd)), _full((r, wd)), _full((r, wd)),
                   _full((kw - 1, ns, w)), _full((ns, w))],
        out_shape=[sd((r, w), BF16), sd((r, wd), F32), sd((r, wd), F32), sd((r, wd), F32), sd((r, wd), F32),
                   sd((kw - 1, ns, w), F32), sd((ns, w), F32)],
        compiler_params=_cparams(("arbitrary",)),
        name="odd_proj_sample",
    )(*args)


def _paged_attn_kernel(pt_ref, q_ref, kn_ref, vn_ref, dg_ref, *refs, nh, ts, pps, page, nblk, nbp):
    k_refs = refs[0:pps]
    v_refs = refs[pps:2 * pps]
    o_ref = refs[2 * pps]
    gate_scr, m_scr, l_scr, o_scr = refs[2 * pps + 1:]
    s_idx = pl.program_id(1)
    nsteps = pl.num_programs(1)
    rows = ts * nh
    wd = q_ref.shape[2]
    ppb = MOBA_BLOCK // page
    bps = pps // ppb

    hm = lax.broadcasted_iota(jnp.int32, (nh, wd), 1) // HD_D == lax.broadcasted_iota(jnp.int32, (nh, wd), 0)
    q4 = q_ref[0]
    qexp = jnp.concatenate([jnp.where(hm, jnp.broadcast_to(q4[t:t + 1, :], (nh, wd)), 0.0) for t in range(ts)], axis=0)
    qexp_b = qexp.astype(BF16)
    q2 = jnp.concatenate([qexp_b, (qexp - qexp_b.astype(F32)).astype(BF16)], axis=0)

    @pl.when(s_idx == 0)
    def _():
        gate_scr[...] = jnp.zeros_like(gate_scr)
        m_scr[...] = jnp.zeros_like(m_scr)
        l_scr[...] = jnp.zeros_like(l_scr)

    g_lane = lax.broadcasted_iota(jnp.int32, (rows, nbp), 1)
    row_pad = jnp.zeros((LANES - rows, MOBA_BLOCK), F32)
    kt_all = jnp.concatenate([k_refs[j][0, 0] for j in range(pps)], axis=1).astype(BF16)
    s2_all = _dot(q2, kt_all)
    s_all = s2_all[0:rows] + s2_all[rows:2 * rows]
    for c in range(bps):
        blk = s_idx * bps + c
        vt = jnp.concatenate([v_refs[c * ppb + j][0, 0] for j in range(ppb)], axis=1)
        s = s_all[:, c * MOBA_BLOCK:(c + 1) * MOBA_BLOCK]
        m = jnp.max(s, axis=1, keepdims=True)
        p = jnp.exp(s - m).astype(BF16).astype(F32)
        gate_scr[...] = jnp.where(g_lane == blk, jnp.mean(s, axis=1, keepdims=True), gate_scr[...])
        m_scr[...] = jnp.where(g_lane == blk, m, m_scr[...])
        l_scr[...] = jnp.where(g_lane == blk, jnp.sum(p, axis=1, keepdims=True), l_scr[...])
        p_t = jnp.concatenate([p, row_pad], axis=0).T.astype(BF16)
        o_scr[blk] = _dot(vt.astype(BF16), p_t)

    @pl.when(s_idx == nsteps - 1)
    def _():
        gates = gate_scr[...]
        lane = lax.broadcasted_iota(jnp.int32, gates.shape, 1)
        neg_inf = jnp.float32(-jnp.inf)
        g = jnp.where(lane < nblk, gates, neg_inf)
        sel = jnp.zeros(gates.shape, jnp.bool_)
        for _ in range(MOBA_TOPK):
            mx = jnp.max(g, axis=1, keepdims=True)
            idx = jnp.min(jnp.where(g == mx, lane, jnp.int32(nbp)), axis=1, keepdims=True)
            pick = (lane == idx) & (mx > neg_inf)
            sel = sel | pick
            g = jnp.where(pick, neg_inf, g)

        kn = kn_ref[0]
        vn = vn_ref[0]
        s_own = _dot_nt(qexp_b, kn.astype(BF16))
        tq = lax.broadcasted_iota(jnp.int32, s_own.shape, 0) // nh
        tk = lax.broadcasted_iota(jnp.int32, s_own.shape, 1)
        s_own = jnp.where(tk <= tq, s_own, MASK_NEG)
        m_own = jnp.max(s_own, axis=1, keepdims=True)
        m_blk = m_scr[...]
        m_tot = jnp.maximum(m_own, jnp.max(jnp.where(sel, m_blk, M_INIT), axis=1, keepdims=True))
        p_own = jnp.exp(s_own - m_tot)
        num = jnp.zeros((rows, wd), F32)
        for t in range(ts):
            num = num + p_own[:, t:t + 1] * vn[t:t + 1, :]
        wgt = jnp.where(sel, jnp.exp(jnp.where(sel, m_blk, M_INIT) - m_tot), 0.0)
        den = jnp.sum(p_own, axis=1, keepdims=True) + jnp.sum(wgt * l_scr[...], axis=1, keepdims=True)
        wgt_t = jnp.concatenate([wgt, jnp.zeros((LANES - rows, nbp), F32)], axis=0).T
        num_t = jnp.zeros((wd, LANES), F32)
        for n in range(nblk):
            num_t = num_t + o_scr[n] * wgt_t[n:n + 1, :]
        out = (num + num_t.T[0:rows, :]) / den
        att = jnp.concatenate(
            [jnp.sum(jnp.where(hm, out[t * nh:(t + 1) * nh, :], 0.0), axis=0, keepdims=True) for t in range(ts)], axis=0)
        o_ref[0] = (att * dg_ref[0]).astype(BF16)


def _paged_attn(page_table, q, k_new, v_new, dg, cache_k, cache_v, layer):
    ns, ts, wd = q.shape
    npages = page_table.shape[1]
    page = cache_k.shape[3]
    nh = wd // HD_D
    pps = min(PAGES_PER_STEP, npages)
    ppb = MOBA_BLOCK // page
    assert MOBA_BLOCK % page == 0 and npages % pps == 0 and pps % ppb == 0
    nblk = npages // ppb
    nbp = -(-nblk // LANES) * LANES
    kern = functools.partial(_paged_attn_kernel, nh=nh, ts=ts, pps=pps, page=page, nblk=nblk, nbp=nbp)

    def page_spec(p):
        return pl.BlockSpec((1, 1, wd, page), lambda b, s, pt: (pt[b * npages + s * pps + p], layer, 0, 0))

    seq_spec = pl.BlockSpec((1, ts, wd), lambda b, s, pt: (b, 0, 0))
    rows = ts * nh
    grid_spec = pltpu.PrefetchScalarGridSpec(
        num_scalar_prefetch=1,
        grid=(ns, npages // pps),
        in_specs=[seq_spec, seq_spec, seq_spec, seq_spec] + [page_spec(p) for p in range(pps)] * 2,
        out_specs=seq_spec,
        scratch_shapes=[pltpu.VMEM((rows, nbp), F32), pltpu.VMEM((rows, nbp), F32), pltpu.VMEM((rows, nbp), F32),
                        pltpu.VMEM((nblk, wd, LANES), F32)],
    )
    return pl.pallas_call(
        kern,
        grid_spec=grid_spec,
        out_shape=jax.ShapeDtypeStruct((ns, ts, wd), BF16),
        compiler_params=_cparams(("arbitrary", "arbitrary")),
        name="paged_attn",
    )(page_table.reshape(-1), q, k_new, v_new, dg, *([cache_k] * pps), *([cache_v] * pps))


def _rope_tables(pos):
    half = HD_D // 2
    inv = ROPE_THETA ** (-jnp.arange(half, dtype=F32) / half)
    ang = pos.astype(F32)[:, None] * inv[None, :]
    cos = jnp.cos(ang)
    sin = jnp.sin(ang)
    reps = LANES // HD_D
    return (jnp.concatenate([cos, cos] * reps, axis=1), jnp.concatenate([-sin, sin] * reps, axis=1))


def _block_diag(wb):
    n, d, _ = wb.shape
    eye = jnp.eye(n, dtype=wb.dtype)
    return (eye[:, None, :, None] * wb[:, :, None, :]).reshape(n * d, n * d)


def _row(v):
    return v.reshape(1, -1)


def kernel(x_prompt, x_sample, cache_k, cache_v, page_table, state_conv_a, state_conv_c, state_lru_h, w_in_even,
           conv_a_w, conv_a_b, ln_a_g, ln_a_b, ln_v_g, ln_v_b, gmlp_w, gmlp_b, w_out_even, post_g_even, post_b_even,
           w_in_odd, conv_c_w, conv_c_b, lru_wr, lru_br, lru_wi, lru_bi, lru_lambda, w_out_odd, post_g_odd,
           post_b_odd):
    n_even, n_odd = w_in_even.shape[0], w_in_odd.shape[0]
    depth = n_even + n_odd
    alpha = (2.0 * depth) ** 0.25
    bsz, t_p, d = x_prompt.shape
    ns, ts, _ = x_sample.shape
    n_pool, _, page, nh, hd = cache_k.shape
    assert hd == HD_D
    wd = nh * hd
    past_len = page_table.shape[1] * page
    assert past_len % MOBA_BLOCK == 0 and t_p % MOBA_BLOCK == 0
    w_b = ln_v_g.shape[1]
    gd = w_b // B_GROUPS

    cache_k2 = jnp.transpose(cache_k, (0, 1, 3, 4, 2)).reshape(n_pool, n_odd, wd, page)
    cache_v2 = jnp.transpose(cache_v, (0, 1, 3, 4, 2)).reshape(n_pool, n_odd, wd, page)
    time_major_heads = lambda a: jnp.transpose(a.reshape(bsz, nh, hd, t_p), (0, 3, 1, 2))
    cos_p, sin_p = _rope_tables(jnp.arange(t_p))
    cos_s, sin_s = _rope_tables(past_len + jnp.repeat(jnp.arange(ts), ns))
    nb_p = t_p // MOBA_BLOCK
    nbp_p = -(-nb_p // LANES) * LANES
    kc = min(MOBA_KEY_CHUNK, t_p)
    onehot = (jnp.arange(nbp_p)[None, :, None] == (jnp.arange(t_p) // MOBA_BLOCK).reshape(t_p // kc, 1, kc)).astype(BF16)

    xp = x_prompt
    xs = jnp.swapaxes(x_sample, 0, 1).reshape(ts * ns, d)
    tmaj = lambda a: jnp.swapaxes(a, 0, 1)
    outs = {k: [] for k in ("ca_p", "ca_s", "gv_s", "cc_p", "cc_s", "h_p", "h_s", "k_p", "v_p", "k_s", "v_s")}
    tc_s = min(ts, GMLP_CHUNK)
    assert ts % tc_s == 0 and ts == tc_s

    for l in range(depth):
        j = l // 2
        if l % 2 == 0:
            gm_bias_p = jnp.repeat(gmlp_b[j].T, gd, axis=1)
            prm_p = (w_in_even[j].astype(BF16), conv_a_w[j], _row(conv_a_b[j]), _row(ln_a_g[j]), _row(ln_a_b[j]),
                     _row(ln_v_g[j]), _row(ln_v_b[j]), gmlp_w[j], gm_bias_p, w_out_even[j].astype(BF16),
                     _row(post_g_even[j]), _row(post_b_even[j]))
            xp, buf_p = _even_prompt(xp, prm_p, alpha)
            wexp = jnp.repeat(jnp.transpose(gmlp_w[j][:, :ts, :ts], (1, 2, 0)).reshape(ts * ts, B_GROUPS), gd, axis=1)
            bexp = jnp.repeat(gmlp_b[j][:, :ts].T, gd, axis=1)
            prm_s = prm_p[:7] + (wexp, bexp) + prm_p[9:]
            xs, buf_s, gv = _even_sample(xs, tmaj(state_conv_a[:, j]), prm_s, alpha, ns, ts)
            outs["ca_p"].append(buf_p)
            outs["ca_s"].append(tmaj(buf_s))
            outs["gv_s"].append(tmaj(gv.reshape(ts, ns, w_b)))
        else:
            w_gates = jnp.concatenate([_block_diag(lru_wr[j]), _block_diag(lru_wi[j])], axis=1).astype(BF16)
            prm = (w_in_odd[j].astype(BF16), conv_c_w[j], _row(conv_c_b[j]), w_gates, _row(lru_br[j]),
                   _row(lru_bi[j]), _row(lru_lambda[j]))
            w_out = w_out_odd[j].astype(BF16)
            pg, pb = _row(post_g_odd[j]), _row(post_b_odd[j])
            c_out, q, k_t, v_t, ktb, vb, dg, kmean, cb_p, hl_p = _odd_proj_prompt(xp, prm, cos_p, sin_p)
            kmean = jnp.pad(kmean, ((0, 0), (0, nbp_p - nb_p), (0, 0)))
            d_out = _moba_prompt(q, ktb, vb, onehot, kmean, dg)
            xp = _out_proj(xp, c_out, d_out, w_out, pg, pb, alpha)
            outs["cc_p"].append(cb_p)
            outs["h_p"].append(hl_p[:, 0])
            outs["k_p"].append(time_major_heads(k_t))
            outs["v_p"].append(time_major_heads(v_t))
            c_out_s, q_s, k_s, v_s, dg_s, cb_s, hl_s = _odd_proj_sample(
                xs, tmaj(state_conv_c[:, j]), state_lru_h[:, j], prm, cos_s, sin_s, ns, ts)
            seq = lambda a: tmaj(a.reshape(ts, ns, wd))
            k_seq, v_seq = seq(k_s), seq(v_s)
            att = _paged_attn(page_table, seq(q_s), k_seq, v_seq, seq(dg_s), cache_k2, cache_v2, j)
            d_out_s = tmaj(att).reshape(ts * ns, wd)
            xs = _out_proj(xs[None], c_out_s[None], d_out_s[None], w_out, pg, pb, alpha)[0]
            outs["cc_s"].append(tmaj(cb_s))
            outs["h_s"].append(hl_s)
            outs["k_s"].append(k_seq.reshape(ns, ts, nh, hd))
            outs["v_s"].append(v_seq.reshape(ns, ts, nh, hd))

    st = lambda name: jnp.stack(outs[name], axis=1)
    return (xp, tmaj(xs.reshape(ts, ns, d)), st("ca_p"), st("ca_s"), st("gv_s"), st("cc_p"), st("cc_s"),
            st("h_p"), st("h_s"), st("k_p"), st("v_p"), st("k_s"), st("v_s"))
```

```python
import functools
import math

import jax
import jax.numpy as jnp
import numpy as np
from jax import lax
from jax.experimental import pallas as pl
from jax.experimental.pallas import tpu as pltpu

F32 = jnp.float32
BF16 = jnp.bfloat16

GMLP_CHUNK = 128
B_GROUPS = 4
C_BLOCKS = 8
LRU_C = 8.0
HD_D = 64
MOBA_BLOCK = 256
MOBA_TOPK = 3
ROPE_THETA = 10000.0
LN_EPS = 1e-5

LANES = 128
SUBLANES = 8
VMEM_LIMIT_BYTES = 56 * 1024 * 1024

MASK_NEG = -1e30
M_INIT = -1e29

PROMPT_TM = 512
CONV_HALO = 32
PAGES_PER_STEP = 32
MOBA_KEY_CHUNK = 1024
MOBA_Q_TILE = 1024


def _ln(x, g, b):
    mu = jnp.mean(x, axis=-1, keepdims=True)
    xc = x - mu
    var = jnp.mean(xc * xc, axis=-1, keepdims=True)
    return xc * lax.rsqrt(var + LN_EPS) * g + b


def _sigmoid(x):
    return jax.nn.sigmoid(x)


def _silu(x):
    return x * jax.nn.sigmoid(x)


def _gelu(x):
    c = math.sqrt(2.0 / math.pi)
    return x * (0.5 * (1.0 + jnp.tanh(c * (x + 0.044715 * (x * x * x)))))


def _softplus(y):
    return jnp.maximum(y, 0.0) + jnp.log1p(jnp.exp(-jnp.abs(y)))


def _dot(a, b):
    return jnp.dot(a, b, preferred_element_type=F32)


def _dot_nt(a, b, precision=None):
    return lax.dot_general(a, b, (((1,), (1,)), ((), ())), preferred_element_type=F32, precision=precision)


def _cparams(sem):
    return pltpu.CompilerParams(dimension_semantics=sem, vmem_limit_bytes=VMEM_LIMIT_BYTES)


def _full(shape):
    n = len(shape)
    return pl.BlockSpec(shape, lambda *_: (0,) * n)


def _zero_conv_halo(scr):
    scr[:, 0:CONV_HALO, :] = jnp.zeros((scr.shape[0], CONV_HALO, LANES), F32)


def _causal_conv_tile(scr, a_in, cw_ref, cb_ref, buf_ref, tm, kw, after_slab=None):
    w = a_in.shape[1]
    nl = w // LANES
    out = []
    for j in range(nl):
        lanes = slice(j * LANES, (j + 1) * LANES)
        scr[j, CONV_HALO:CONV_HALO + tm, :] = a_in[:, lanes]
        acc = jnp.zeros((tm, LANES), F32) + cb_ref[:, lanes]
        for k in range(kw):
            acc = acc + cw_ref[k:k + 1, lanes] * scr[j, pl.ds(CONV_HALO - (kw - 1) + k, tm), :]
        out.append(acc)
        buf_ref[0, :, lanes] = scr[j, pl.ds(CONV_HALO + tm - (kw - 1), kw - 1), :]
        scr[j, 0:CONV_HALO, :] = scr[j, pl.ds(tm, CONV_HALO), :]
        if after_slab is not None:
            after_slab(j)
    return jnp.concatenate(out, axis=1)


def _even_prompt_kernel(x_ref, win_ref, cw_ref, cb_ref, lag_ref, lab_ref, lvg_ref, lvb_ref, gmw_ref, gmb_ref,
                        wout_ref, pg_ref, pb_ref, y_ref, buf_ref, scr, *, alpha, tm, w, kw):
    @pl.when(pl.program_id(1) == 0)
    def _():
        _zero_conv_halo(scr)

    x = x_ref[0]
    z = _dot(x.astype(BF16), win_ref[...])
    a_val, a_glu, a_gate = z[:, 0:w], z[:, w:2 * w], z[:, 2 * w:3 * w]
    b_u, b_v, b_gate = z[:, 3 * w:4 * w], z[:, 4 * w:5 * w], z[:, 5 * w:6 * w]

    a_in = a_val * _sigmoid(a_glu)
    a_conv = _causal_conv_tile(scr, a_in, cw_ref, cb_ref, buf_ref, tm, kw)
    a_out = _silu(_ln(a_conv, lag_ref[...], lab_ref[...])) * _silu(a_gate)

    u = _gelu(b_u)
    v = _ln(_gelu(b_v), lvg_ref[...], lvb_ref[...]).astype(BF16)
    gd = w // B_GROUPS
    ri = lax.broadcasted_iota(jnp.int32, (GMLP_CHUNK, GMLP_CHUNK), 0)
    ci = lax.broadcasted_iota(jnp.int32, (GMLP_CHUNK, GMLP_CHUNK), 1)
    wmix = [jnp.where(ri >= ci, gmw_ref[g], 0.0).astype(BF16) for g in range(B_GROUPS)]
    rows = []
    for c in range(tm // GMLP_CHUNK):
        vc = v[c * GMLP_CHUNK:(c + 1) * GMLP_CHUNK, :]
        rows.append(jnp.concatenate(
            [_dot(wmix[g], vc[:, g * gd:(g + 1) * gd]) for g in range(B_GROUPS)], axis=1) + gmb_ref[...])
    mixed = jnp.concatenate(rows, axis=0)
    b_out = u * mixed * _silu(b_gate)

    y = _dot(a_out.astype(BF16), wout_ref[0:w, :]) + _dot(b_out.astype(BF16), wout_ref[w:2 * w, :])
    y_ref[0] = _ln(alpha * x + y, pg_ref[...], pb_ref[...])


def _even_prompt(x, prm, alpha):
    (w_in, conv_w, conv_b, ln_a_g, ln_a_b, ln_v_g, ln_v_b, gm_w, gm_bias, w_out, pg, pb) = prm
    bsz, t, d = x.shape
    tm = min(PROMPT_TM, t)
    w = conv_w.shape[1]
    kw = conv_w.shape[0]
    assert t % tm == 0 and tm % GMLP_CHUNK == 0 and kw - 1 <= CONV_HALO <= tm
    kern = functools.partial(_even_prompt_kernel, alpha=alpha, tm=tm, w=w, kw=kw)
    return pl.pallas_call(
        kern,
        grid=(bsz, t // tm),
        in_specs=[pl.BlockSpec((1, tm, d), lambda b, i: (b, i, 0)),
                  _full(w_in.shape), _full(conv_w.shape), _full(conv_b.shape), _full(ln_a_g.shape),
                  _full(ln_a_b.shape), _full(ln_v_g.shape), _full(ln_v_b.shape), _full(gm_w.shape),
                  _full(gm_bias.shape), _full(w_out.shape), _full(pg.shape), _full(pb.shape)],
        out_specs=[pl.BlockSpec((1, tm, d), lambda b, i: (b, i, 0)),
                   pl.BlockSpec((1, kw - 1, w), lambda b, i: (b, 0, 0))],
        out_shape=[jax.ShapeDtypeStruct((bsz, t, d), F32), jax.ShapeDtypeStruct((bsz, kw - 1, w), F32)],
        scratch_shapes=[pltpu.VMEM((w // LANES, CONV_HALO + tm, LANES), F32)],
        compiler_params=_cparams(("arbitrary", "arbitrary")),
        name="even_prompt",
    )(x, w_in, conv_w, conv_b, ln_a_g, ln_a_b, ln_v_g, ln_v_b, gm_w, gm_bias, w_out, pg, pb)


def _rope(x, cos, sin_signed):
    n = x.shape[1]
    reps = n // LANES
    cos_f = jnp.concatenate([cos] * reps, axis=1)
    sin_f = jnp.concatenate([sin_signed] * reps, axis=1)
    lane = lax.broadcasted_iota(jnp.int32, x.shape, 1)
    first = (lane % HD_D) < (HD_D // 2)
    rot = jnp.where(first, pltpu.roll(x, n - HD_D // 2, 1), pltpu.roll(x, HD_D // 2, 1))
    return x * cos_f + rot * sin_f


def _lru_coeffs(xc, wg_ref, br, bi, lam, w):
    gates = _dot(xc.astype(BF16), wg_ref[...])
    r = _sigmoid(gates[:, 0:w] + br)
    ig = _sigmoid(gates[:, w:2 * w] + bi)
    log_a = (-LRU_C) * r * _softplus(-lam)
    a = jnp.exp(log_a)
    bx = jnp.sqrt(1.0 - a * a) * (ig * xc)
    return a, bx


def _scan_tile(a, b, h0):
    tm = a.shape[0]
    row = lax.broadcasted_iota(jnp.int32, a.shape, 0) % SUBLANES
    s = 1
    while s < SUBLANES:
        a_sh = pltpu.roll(a, s, 0)
        b_sh = pltpu.roll(b, s, 0)
        ok = row >= s
        b = jnp.where(ok, a * b_sh + b, b)
        a = jnp.where(ok, a * a_sh, a)
        s *= 2
    h = h0
    hs = []
    for g in range(tm // SUBLANES):
        rows = slice(g * SUBLANES, (g + 1) * SUBLANES)
        hg = b[rows] + a[rows] * h
        hs.append(hg)
        h = hg[SUBLANES - 1:SUBLANES]
    return jnp.concatenate(hs, axis=0), h


def _odd_proj_prompt_kernel(x_ref, win_ref, cw_ref, cb_ref, wg_ref, br_ref, bi_ref, lam_ref, cos_ref, sin_ref,
                            cout_ref, q_ref, kt_ref, vt_ref, ktb_ref, vb_ref, dg_ref, kmean_ref, buf_ref, hl_ref,
                            scr, h_scr, *, tm, w, kw, scale):
    t = pl.program_id(1)

    @pl.when(t == 0)
    def _():
        _zero_conv_halo(scr)
        h_scr[...] = jnp.zeros_like(h_scr)

    x = x_ref[0]
    z = _dot(x.astype(BF16), win_ref[...])
    c_x, c_gate = z[:, 0:w], z[:, w:2 * w]
    wd = (z.shape[1] - 2 * w) // 4
    o = 2 * w
    q, k, v, d_gate = z[:, o:o + wd], z[:, o + wd:o + 2 * wd], z[:, o + 2 * wd:o + 3 * wd], z[:, o + 3 * wd:o + 4 * wd]

    xc = _causal_conv_tile(scr, c_x, cw_ref, cb_ref, buf_ref, tm, kw)
    a, bx = _lru_coeffs(xc, wg_ref, br_ref[...], bi_ref[...], lam_ref[...], w)
    h, h_last = _scan_tile(a, bx, h_scr[0:1, :])
    h_scr[0:1, :] = h_last
    hl_ref[0] = h_last
    cout_ref[0] = (h * _silu(c_gate)).astype(BF16)

    cos = cos_ref[...]
    sin = sin_ref[...]
    q_r = _rope(q, cos, sin)
    k_r = _rope(k, cos, sin)
    q_ref[0] = q_r * scale
    k_t = k_r.T
    kt_ref[0] = k_t
    vt_ref[0] = v.T
    vb_ref[0] = v.astype(BF16)
    dg_ref[0] = _silu(d_gate)
    ktb_ref[0, 0] = k_t.astype(BF16)
    nblk = tm // MOBA_BLOCK
    for c in range(nblk):
        kmean_ref[0, pl.ds(t * nblk + c, 1), :] = jnp.mean(
            k_r[c * MOBA_BLOCK:(c + 1) * MOBA_BLOCK, :], axis=0, keepdims=True)


def _odd_proj_prompt(x, prm, cos, sin):
    (w_in, conv_w, conv_b, w_gates, br, bi, lam) = prm
    bsz, t, d = x.shape
    tm = min(PROMPT_TM, t)
    w = conv_w.shape[1]
    kw = conv_w.shape[0]
    wd = (w_in.shape[1] - 2 * w) // 4
    nb = t // MOBA_BLOCK
    kc = min(MOBA_KEY_CHUNK, t)
    tpc = kc // tm
    assert t % tm == 0 and tm % MOBA_BLOCK == 0 and kw - 1 <= CONV_HALO <= tm and kc % tm == 0 and t % kc == 0
    kern = functools.partial(_odd_proj_prompt_kernel, tm=tm, w=w, kw=kw, scale=HD_D ** -0.5)
    row_blk = lambda n: pl.BlockSpec((1, tm, n), lambda b, i: (b, i, 0))
    per_b = lambda r, n: pl.BlockSpec((1, r, n), lambda b, i: (b, 0, 0))
    col_blk = pl.BlockSpec((1, wd, tm), lambda b, i: (b, 0, i))
    return pl.pallas_call(
        kern,
        grid=(bsz, t // tm),
        in_specs=[row_blk(d), _full(w_in.shape), _full(conv_w.shape), _full(conv_b.shape), _full(w_gates.shape),
                  _full(br.shape), _full(bi.shape), _full(lam.shape),
                  pl.BlockSpec((tm, LANES), lambda b, i: (i, 0)), pl.BlockSpec((tm, LANES), lambda b, i: (i, 0))],
        out_specs=[row_blk(w), row_blk(wd), col_blk, col_blk,
                   pl.BlockSpec((1, 1, wd, tm), lambda b, i: (b, i // tpc, 0, i % tpc)),
                   row_blk(wd), row_blk(wd), per_b(nb, wd), per_b(kw - 1, w), per_b(1, w)],
        out_shape=[jax.ShapeDtypeStruct((bsz, t, w), BF16),
                   jax.ShapeDtypeStruct((bsz, t, wd), F32),
                   jax.ShapeDtypeStruct((bsz, wd, t), F32),
                   jax.ShapeDtypeStruct((bsz, wd, t), F32),
                   jax.ShapeDtypeStruct((bsz, t // kc, wd, kc), BF16),
                   jax.ShapeDtypeStruct((bsz, t, wd), BF16),
                   jax.ShapeDtypeStruct((bsz, t, wd), F32),
                   jax.ShapeDtypeStruct((bsz, nb, wd), F32),
                   jax.ShapeDtypeStruct((bsz, kw - 1, w), F32),
                   jax.ShapeDtypeStruct((bsz, 1, w), F32)],
        scratch_shapes=[pltpu.VMEM((w // LANES, CONV_HALO + tm, LANES), F32), pltpu.VMEM((SUBLANES, w), F32)],
        compiler_params=_cparams(("arbitrary", "arbitrary")),
        name="odd_proj_prompt",
    )(x, w_in, conv_w, conv_b, w_gates, br, bi, lam, cos, sin)


def _select_topk_bias(gt, own):
    row = lax.broadcasted_iota(jnp.int32, gt.shape, 0)
    neg_inf = jnp.float32(-jnp.inf)
    g = jnp.where(row < own, gt, neg_inf)
    sel = row == own
    big = jnp.int32(gt.shape[0])
    for _ in range(MOBA_TOPK):
        mx = jnp.max(g, axis=0, keepdims=True)
        idx = jnp.min(jnp.where(g == mx, row, big), axis=0, keepdims=True)
        pick = (row == idx) & (mx > neg_inf)
        sel = sel | pick
        g = jnp.where(pick, neg_inf, g)
    return jnp.where(sel, 0.0, MASK_NEG)


def _moba_prompt_kernel(q_ref, kt_ref, v_ref, oh_ref, kmean_ref, dg_ref, o_ref, acc_ref, m_ref, *, nbp, nbr, kc, bq):
    g = pl.program_id(2)
    own = (g * bq + lax.broadcasted_iota(jnp.int32, (1, bq), 1)) // MOBA_BLOCK
    q = q_ref[0]
    lane = lax.broadcasted_iota(jnp.int32, (bq, LANES), 1)
    head_lanes = [lane < HD_D, lane >= HD_D]
    vlane = lax.broadcasted_iota(jnp.int32, (kc, LANES), 1)
    v_head = [vlane < HD_D, vlane >= HD_D]
    v_one = [(vlane == HD_D).astype(BF16), (vlane == 0).astype(BF16)]
    kml = lax.broadcasted_iota(jnp.int32, (nbr, LANES), 1)
    km_heads = [kml < HD_D, kml >= HD_D]
    kmean = kmean_ref[0, 0:nbr, :]

    qaug = []
    for h in range(2):
        gt = _dot_nt(jnp.where(km_heads[h], kmean, 0.0), q, precision=lax.Precision.HIGHEST)
        bias_t = _select_topk_bias(gt, own)
        if nbp > nbr:
            bias_t = jnp.concatenate([bias_t, jnp.full((nbp - nbr, bq), MASK_NEG, F32)], axis=0)
        qh = jnp.where(head_lanes[h], q, 0.0).astype(BF16)
        qaug.append(jnp.concatenate([qh, bias_t.T.astype(BF16)], axis=1))

    acc_ref[...] = jnp.zeros_like(acc_ref)
    m_ref[...] = jnp.full(m_ref.shape, M_INIT, F32)

    def chunk(c, causal):
        start = pl.multiple_of(c * kc, kc)
        vn = v_ref[0, pl.ds(start, kc), :]
        kaug = jnp.concatenate([kt_ref[0, c], oh_ref[c]], axis=0)
        for h in range(2):
            s = _dot(qaug[h], kaug)
            if causal:
                qpos = lax.broadcasted_iota(jnp.int32, (bq, kc), 0) + g * bq
                kpos = lax.broadcasted_iota(jnp.int32, (bq, kc), 1) + c * kc
                s = jnp.where(kpos <= qpos, s, MASK_NEG)
            m_prev = m_ref[h]
            m_new = jnp.maximum(m_prev, jnp.max(s, axis=1, keepdims=True))
            p = jnp.exp(s - jnp.concatenate([m_new] * (kc // LANES), axis=1))
            alpha = jnp.exp(m_prev - m_new)
            vh = jnp.where(v_head[h], vn, v_one[h])
            acc_ref[h] = alpha * acc_ref[h] + _dot(p.astype(BF16), vh)
            m_ref[h] = m_new

    last = ((g + 1) * bq - 1) // kc

    def body(c, carry):
        chunk(c, False)
        return carry

    lax.fori_loop(0, last, body, 0)
    chunk(last, True)

    acc_a = acc_ref[0]
    acc_b = acc_ref[1]
    out = jnp.where(head_lanes[0], acc_a / acc_a[:, HD_D:HD_D + 1], acc_b / acc_b[:, 0:1])
    o_ref[0] = (out * dg_ref[0]).astype(BF16)


def _moba_prompt(q, ktc, vb, onehot, kmean, dg):
    bsz, t, wd = q.shape
    nq = t // MOBA_BLOCK
    npairs = wd // LANES
    nbp = kmean.shape[1]
    nc, _, kc = ktc.shape[1:]
    nbr = -(-nq // SUBLANES) * SUBLANES
    bq = min(MOBA_Q_TILE, kc)
    assert t % MOBA_BLOCK == 0 and wd % LANES == 0 and nbp % LANES == 0 and nq <= nbp and kc % bq == 0
    assert bq % MOBA_BLOCK == 0
    kern = functools.partial(_moba_prompt_kernel, nbp=nbp, nbr=nbr, kc=kc, bq=bq)
    qblk = pl.BlockSpec((1, bq, LANES), lambda b, p, i: (b, i, p))
    seq = lambda r: pl.BlockSpec((1, r, LANES), lambda b, p, i: (b, 0, p))
    ktblk = pl.BlockSpec((1, nc, LANES, kc), lambda b, p, i: (b, 0, p, 0))
    return pl.pallas_call(
        kern,
        grid=(bsz, npairs, t // bq),
        in_specs=[qblk, ktblk, seq(t), _full(onehot.shape), seq(nbp), qblk],
        out_specs=qblk,
        out_shape=jax.ShapeDtypeStruct((bsz, t, wd), BF16),
        scratch_shapes=[pltpu.VMEM((2, bq, LANES), F32), pltpu.VMEM((2, bq, LANES), F32)],
        compiler_params=_cparams(("arbitrary", "arbitrary", "arbitrary")),
        name="moba_prompt",
    )(q, ktc, vb, onehot, kmean, dg)


def _out_proj_kernel(x_ref, c_ref, d_ref, wout_ref, pg_ref, pb_ref, y_ref, *, alpha, w):
    x = x_ref[0]
    y = _dot(c_ref[0], wout_ref[0:w, :]) + _dot(d_ref[0], wout_ref[w:, :])
    y_ref[0] = _ln(alpha * x + y, pg_ref[...], pb_ref[...])


def _out_proj(x, c_out, d_out, w_out, pg, pb, alpha):
    bsz, t, d = x.shape
    tm = min(PROMPT_TM, t)
    w = c_out.shape[2]
    assert t % tm == 0
    kern = functools.partial(_out_proj_kernel, alpha=alpha, w=w)
    blk = lambda n: pl.BlockSpec((1, tm, n), lambda b, i: (b, i, 0))
    return pl.pallas_call(
        kern,
        grid=(bsz, t // tm),
        in_specs=[blk(d), blk(w), blk(d_out.shape[2]), _full(w_out.shape), _full(pg.shape), _full(pb.shape)],
        out_specs=blk(d),
        out_shape=jax.ShapeDtypeStruct((bsz, t, d), F32),
        compiler_params=_cparams(("arbitrary", "arbitrary")),
        name="out_proj",
    )(x, c_out, d_out, w_out, pg, pb)


def _sample_conv(state_ref, a_in, cw_ref, cb, ns, ts, kw):
    def xp(s):
        return state_ref[s] if s < kw - 1 else a_in[(s - (kw - 1)) * ns:(s - (kw - 2)) * ns, :]

    out = []
    for t in range(ts):
        acc = jnp.zeros((ns, a_in.shape[1]), F32) + cb
        for k in range(kw):
            acc = acc + cw_ref[k:k + 1, :] * xp(t + k)
        out.append(acc)
    return jnp.concatenate(out, axis=0), [xp(ts + s) for s in range(kw - 1)]


def _even_sample_kernel(x_ref, st_ref, win_ref, cw_ref, cb_ref, lag_ref, lab_ref, lvg_ref, lvb_ref, wexp_ref,
                        bexp_ref, wout_ref, pg_ref, pb_ref, y_ref, buf_ref, gv_ref, *, alpha, ns, ts, w, kw):
    x = x_ref[...]
    z = _dot(x.astype(BF16), win_ref[...])
    a_val, a_glu, a_gate = z[:, 0:w], z[:, w:2 * w], z[:, 2 * w:3 * w]
    b_u, b_v, b_gate = z[:, 3 * w:4 * w], z[:, 4 * w:5 * w], z[:, 5 * w:6 * w]
    a_in = a_val * _sigmoid(a_glu)
    a_conv, new_state = _sample_conv(st_ref, a_in, cw_ref, cb_ref[...], ns, ts, kw)
    for s in range(kw - 1):
        buf_ref[s] = new_state[s]
    a_out = _silu(_ln(a_conv, lag_ref[...], lab_ref[...])) * _silu(a_gate)

    u = _gelu(b_u)
    v = _ln(_gelu(b_v), lvg_ref[...], lvb_ref[...])
    gv_ref[...] = v
    mixed = []
    for t in range(ts):
        acc = jnp.zeros((ns, w), F32)
        for s in range(t + 1):
            acc = acc + wexp_ref[t * ts + s:t * ts + s + 1, :] * v[s * ns:(s + 1) * ns, :]
        mixed.append(acc + bexp_ref[t:t + 1, :])
    b_out = u * jnp.concatenate(mixed, axis=0) * _silu(b_gate)
    y = _dot(a_out.astype(BF16), wout_ref[0:w, :]) + _dot(b_out.astype(BF16), wout_ref[w:2 * w, :])
    y_ref[...] = _ln(alpha * x + y, pg_ref[...], pb_ref[...])


def _even_sample(x, state, prm, alpha, ns, ts):
    (w_in, conv_w, conv_b, ln_a_g, ln_a_b, ln_v_g, ln_v_b, wexp, bexp, w_out, pg, pb) = prm
    r, d = x.shape
    w = conv_w.shape[1]
    kw = conv_w.shape[0]
    kern = functools.partial(_even_sample_kernel, alpha=alpha, ns=ns, ts=ts, w=w, kw=kw)
    args = (x, state, w_in, conv_w, conv_b, ln_a_g, ln_a_b, ln_v_g, ln_v_b, wexp, bexp, w_out, pg, pb)
    return pl.pallas_call(
        kern,
        grid=(1,),
        in_specs=[_full(a.shape) for a in args],
        out_specs=[_full((r, d)), _full((kw - 1, ns, w)), _full((r, w))],
        out_shape=[jax.ShapeDtypeStruct((r, d), F32), jax.ShapeDtypeStruct((kw - 1, ns, w), F32),
                   jax.ShapeDtypeStruct((r, w), F32)],
        compiler_params=_cparams(("arbitrary",)),
        name="even_sample",
    )(*args)


def _odd_proj_sample_kernel(x_ref, st_ref, h0_ref, win_ref, cw_ref, cb_ref, wg_ref, br_ref, bi_ref, lam_ref,
                            cos_ref, sin_ref, cout_ref, q_ref, k_ref, v_ref, dg_ref, buf_ref, hl_ref,
                            *, ns, ts, w, kw, scale):
    x = x_ref[...]
    z = _dot(x.astype(BF16), win_ref[...])
    c_x, c_gate = z[:, 0:w], z[:, w:2 * w]
    wd = (z.shape[1] - 2 * w) // 4
    o = 2 * w
    q, k, v, d_gate = z[:, o:o + wd], z[:, o + wd:o + 2 * wd], z[:, o + 2 * wd:o + 3 * wd], z[:, o + 3 * wd:o + 4 * wd]
    xc, new_state = _sample_conv(st_ref, c_x, cw_ref, cb_ref[...], ns, ts, kw)
    for s in range(kw - 1):
        buf_ref[s] = new_state[s]
    a, bx = _lru_coeffs(xc, wg_ref, br_ref[...], bi_ref[...], lam_ref[...], w)
    h = h0_ref[...]
    hs = []
    for t in range(ts):
        h = a[t * ns:(t + 1) * ns, :] * h + bx[t * ns:(t + 1) * ns, :]
        hs.append(h)
    hl_ref[...] = h
    cout_ref[...] = (jnp.concatenate(hs, axis=0) * _silu(c_gate)).astype(BF16)
    cos = cos_ref[...]
    sin = sin_ref[...]
    q_ref[...] = _rope(q, cos, sin) * scale
    k_ref[...] = _rope(k, cos, sin)
    v_ref[...] = v
    dg_ref[...] = _silu(d_gate)


def _odd_proj_sample(x, state, h0, prm, cos, sin, ns, ts):
    (w_in, conv_w, conv_b, w_gates, br, bi, lam) = prm
    r, d = x.shape
    w = conv_w.shape[1]
    kw = conv_w.shape[0]
    wd = (w_in.shape[1] - 2 * w) // 4
    kern = functools.partial(_odd_proj_sample_kernel, ns=ns, ts=ts, w=w, kw=kw, scale=HD_D ** -0.5)
    args = (x, state, h0, w_in, conv_w, conv_b, w_gates, br, bi, lam, cos, sin)
    sd = jax.ShapeDtypeStruct
    return pl.pallas_call(
        kern,
        grid=(1,),
        in_specs=[_full(a.shape) for a in args],
        out_specs=[_full((r, w)), _full((r, wd)), _full((r, wd)), _full((r, wd)), _full((r, wd)),
                   _full((kw - 1, ns, w)), _full((ns, w))],
        out_shape=[sd((r, w), BF16), sd((r, wd), F32), sd((r, wd), F32), sd((r, wd), F32), sd((r, wd), F32),
                   sd((kw - 1, ns, w), F32), sd((ns, w), F32)],
        compiler_params=_cparams(("arbitrary",)),
        name="odd_proj_sample",
    )(*args)


def _paged_attn_kernel(pt_ref, q_ref, kn_ref, vn_ref, dg_ref, *refs, nh, ts, pps, page, nblk, nbp, gb):
    k_refs = refs[0:pps]
    v_refs = refs[pps:2 * pps]
    o_ref = refs[2 * pps]
    gate_scr, m_scr, l_scr, o_scr = refs[2 * pps + 1:]
    s_idx = pl.program_id(1)
    nsteps = pl.num_programs(1)
    rows = ts * nh
    wd = q_ref.shape[2]
    ppb = MOBA_BLOCK // page
    bps = pps // ppb

    hm = lax.broadcasted_iota(jnp.int32, (nh, wd), 1) // HD_D == lax.broadcasted_iota(jnp.int32, (nh, wd), 0)
    q4 = q_ref[0]
    qexp = jnp.concatenate([jnp.where(hm, jnp.broadcast_to(q4[t:t + 1, :], (nh, wd)), 0.0) for t in range(ts)], axis=0)
    qexp_b = qexp.astype(BF16)
    q2 = jnp.concatenate([qexp_b, (qexp - qexp_b.astype(F32)).astype(BF16)], axis=0)

    @pl.when(s_idx == 0)
    def _():
        gate_scr[...] = jnp.zeros_like(gate_scr)
        m_scr[...] = jnp.zeros_like(m_scr)
        l_scr[...] = jnp.zeros_like(l_scr)

    g_lane = lax.broadcasted_iota(jnp.int32, (rows, nbp), 1)
    kt_all = jnp.concatenate([k_refs[j][0, 0] for j in range(pps)], axis=1).astype(BF16)
    s2_all = _dot(q2, kt_all)
    s_all = s2_all[0:rows] + s2_all[rows:2 * rows]
    for gi in range(bps // gb):
        p_ts = []
        for j in range(gb):
            c = gi * gb + j
            blk = s_idx * bps + c
            s = s_all[:, c * MOBA_BLOCK:(c + 1) * MOBA_BLOCK]
            m = jnp.max(s, axis=1, keepdims=True)
            p = jnp.exp(s - m).astype(BF16).astype(F32)
            gate_scr[...] = jnp.where(g_lane == blk, jnp.mean(s, axis=1, keepdims=True), gate_scr[...])
            m_scr[...] = jnp.where(g_lane == blk, m, m_scr[...])
            l_scr[...] = jnp.where(g_lane == blk, jnp.sum(p, axis=1, keepdims=True), l_scr[...])
            pieces = [jnp.zeros((j * rows, MOBA_BLOCK), F32)] * (j > 0) + [p]
            pieces += [jnp.zeros((LANES - (j + 1) * rows, MOBA_BLOCK), F32)] * (j < gb - 1)
            p_ts.append(jnp.concatenate(pieces, axis=0).T.astype(BF16))
        first = gi * gb * ppb
        vt_grp = jnp.concatenate([v_refs[first + j][0, 0] for j in range(gb * ppb)], axis=1)
        o_scr[s_idx * (bps // gb) + gi] = _dot(vt_grp.astype(BF16), jnp.concatenate(p_ts, axis=0))

    @pl.when(s_idx == nsteps - 1)
    def _():
        gates = gate_scr[...]
        lane = lax.broadcasted_iota(jnp.int32, gates.shape, 1)
        neg_inf = jnp.float32(-jnp.inf)
        g = jnp.where(lane < nblk, gates, neg_inf)
        sel = jnp.zeros(gates.shape, jnp.bool_)
        for _ in range(MOBA_TOPK):
            mx = jnp.max(g, axis=1, keepdims=True)
            idx = jnp.min(jnp.where(g == mx, lane, jnp.int32(nbp)), axis=1, keepdims=True)
            pick = (lane == idx) & (mx > neg_inf)
            sel = sel | pick
            g = jnp.where(pick, neg_inf, g)

        kn = kn_ref[0]
        vn = vn_ref[0]
        s_own = _dot_nt(qexp_b, kn.astype(BF16))
        tq = lax.broadcasted_iota(jnp.int32, s_own.shape, 0) // nh
        tk = lax.broadcasted_iota(jnp.int32, s_own.shape, 1)
        s_own = jnp.where(tk <= tq, s_own, MASK_NEG)
        m_own = jnp.max(s_own, axis=1, keepdims=True)
        m_blk = m_scr[...]
        m_tot = jnp.maximum(m_own, jnp.max(jnp.where(sel, m_blk, M_INIT), axis=1, keepdims=True))
        p_own = jnp.exp(s_own - m_tot)
        num = jnp.zeros((rows, wd), F32)
        for t in range(ts):
            num = num + p_own[:, t:t + 1] * vn[t:t + 1, :]
        wgt = jnp.where(sel, jnp.exp(jnp.where(sel, m_blk, M_INIT) - m_tot), 0.0)
        den = jnp.sum(p_own, axis=1, keepdims=True) + jnp.sum(wgt * l_scr[...], axis=1, keepdims=True)
        wgt_t = jnp.concatenate([wgt, jnp.zeros((LANES - rows, nbp), F32)], axis=0).T
        blk_row = lax.broadcasted_iota(jnp.int32, wgt_t.shape, 0)
        w_sh = jnp.where(blk_row % gb == 0, wgt_t, 0.0)
        for j in range(1, gb):
            w_sh = w_sh + jnp.where(blk_row % gb == j, pltpu.roll(wgt_t, j * rows, 1), 0.0)
        num_t = jnp.zeros((wd, LANES), F32)
        for g in range(nblk // gb):
            w_row = w_sh[g * gb:g * gb + 1, :]
            for j in range(1, gb):
                w_row = w_row + w_sh[g * gb + j:g * gb + j + 1, :]
            num_t = num_t + o_scr[g] * w_row
        num_g = num_t.T
        for j in range(gb):
            num = num + num_g[j * rows:(j + 1) * rows, :]
        out = num / den
        att = jnp.concatenate(
            [jnp.sum(jnp.where(hm, out[t * nh:(t + 1) * nh, :], 0.0), axis=0, keepdims=True) for t in range(ts)], axis=0)
        o_ref[0] = (att * dg_ref[0]).astype(BF16)


def _paged_attn(page_table, q, k_new, v_new, dg, cache_k, cache_v, layer):
    ns, ts, wd = q.shape
    npages = page_table.shape[1]
    page = cache_k.shape[3]
    nh = wd // HD_D
    pps = min(PAGES_PER_STEP, npages)
    ppb = MOBA_BLOCK // page
    assert MOBA_BLOCK % page == 0 and npages % pps == 0 and pps % ppb == 0
    nblk = npages // ppb
    nbp = -(-nblk // LANES) * LANES
    rows = ts * nh
    gb = LANES // rows
    assert LANES % rows == 0 and (pps // ppb) % gb == 0
    kern = functools.partial(_paged_attn_kernel, nh=nh, ts=ts, pps=pps, page=page, nblk=nblk, nbp=nbp, gb=gb)

    def page_spec(p):
        return pl.BlockSpec((1, 1, wd, page), lambda b, s, pt: (pt[b * npages + s * pps + p], layer, 0, 0))

    seq_spec = pl.BlockSpec((1, ts, wd), lambda b, s, pt: (b, 0, 0))
    rows = ts * nh
    grid_spec = pltpu.PrefetchScalarGridSpec(
        num_scalar_prefetch=1,
        grid=(ns, npages // pps),
        in_specs=[seq_spec, seq_spec, seq_spec, seq_spec] + [page_spec(p) for p in range(pps)] * 2,
        out_specs=seq_spec,
        scratch_shapes=[pltpu.VMEM((rows, nbp), F32), pltpu.VMEM((rows, nbp), F32), pltpu.VMEM((rows, nbp), F32),
                        pltpu.VMEM((nblk // gb, wd, LANES), F32)],
    )
    return pl.pallas_call(
        kern,
        grid_spec=grid_spec,
        out_shape=jax.ShapeDtypeStruct((ns, ts, wd), BF16),
        compiler_params=_cparams(("arbitrary", "arbitrary")),
        name="paged_attn",
    )(page_table.reshape(-1), q, k_new, v_new, dg, *([cache_k] * pps), *([cache_v] * pps))


def _rope_tables(pos):
    half = HD_D // 2
    inv = ROPE_THETA ** (-jnp.arange(half, dtype=F32) / half)
    ang = pos.astype(F32)[:, None] * inv[None, :]
    cos = jnp.cos(ang)
    sin = jnp.sin(ang)
    reps = LANES // HD_D
    return (jnp.concatenate([cos, cos] * reps, axis=1), jnp.concatenate([-sin, sin] * reps, axis=1))


def _block_diag(wb):
    n, d, _ = wb.shape
    eye = jnp.eye(n, dtype=wb.dtype)
    return (eye[:, None, :, None] * wb[:, :, None, :]).reshape(n * d, n * d)


def _row(v):
    return v.reshape(1, -1)


def kernel(x_prompt, x_sample, cache_k, cache_v, page_table, state_conv_a, state_conv_c, state_lru_h, w_in_even,
           conv_a_w, conv_a_b, ln_a_g, ln_a_b, ln_v_g, ln_v_b, gmlp_w, gmlp_b, w_out_even, post_g_even, post_b_even,
           w_in_odd, conv_c_w, conv_c_b, lru_wr, lru_br, lru_wi, lru_bi, lru_lambda, w_out_odd, post_g_odd,
           post_b_odd):
    n_even, n_odd = w_in_even.shape[0], w_in_odd.shape[0]
    depth = n_even + n_odd
    alpha = (2.0 * depth) ** 0.25
    bsz, t_p, d = x_prompt.shape
    ns, ts, _ = x_sample.shape
    n_pool, _, page, nh, hd = cache_k.shape
    assert hd == HD_D
    wd = nh * hd
    past_len = page_table.shape[1] * page
    assert past_len % MOBA_BLOCK == 0 and t_p % MOBA_BLOCK == 0
    w_b = ln_v_g.shape[1]
    gd = w_b // B_GROUPS

    cache_k2 = jnp.transpose(cache_k, (0, 1, 3, 4, 2)).reshape(n_pool, n_odd, wd, page)
    cache_v2 = jnp.transpose(cache_v, (0, 1, 3, 4, 2)).reshape(n_pool, n_odd, wd, page)
    time_major_heads = lambda a: jnp.transpose(a.reshape(bsz, nh, hd, t_p), (0, 3, 1, 2))
    cos_p, sin_p = _rope_tables(jnp.arange(t_p))
    cos_s, sin_s = _rope_tables(past_len + jnp.repeat(jnp.arange(ts), ns))
    nb_p = t_p // MOBA_BLOCK
    nbp_p = -(-nb_p // LANES) * LANES
    kc = min(MOBA_KEY_CHUNK, t_p)
    onehot = (jnp.arange(nbp_p)[None, :, None] == (jnp.arange(t_p) // MOBA_BLOCK).reshape(t_p // kc, 1, kc)).astype(BF16)

    xp = x_prompt
    xs = jnp.swapaxes(x_sample, 0, 1).reshape(ts * ns, d)
    tmaj = lambda a: jnp.swapaxes(a, 0, 1)
    outs = {k: [] for k in ("ca_p", "ca_s", "gv_s", "cc_p", "cc_s", "h_p", "h_s", "k_p", "v_p", "k_s", "v_s")}
    tc_s = min(ts, GMLP_CHUNK)
    assert ts % tc_s == 0 and ts == tc_s

    for l in range(depth):
        j = l // 2
        if l % 2 == 0:
            gm_bias_p = jnp.repeat(gmlp_b[j].T, gd, axis=1)
            prm_p = (w_in_even[j].astype(BF16), conv_a_w[j], _row(conv_a_b[j]), _row(ln_a_g[j]), _row(ln_a_b[j]),
                     _row(ln_v_g[j]), _row(ln_v_b[j]), gmlp_w[j], gm_bias_p, w_out_even[j].astype(BF16),
                     _row(post_g_even[j]), _row(post_b_even[j]))
            xp, buf_p = _even_prompt(xp, prm_p, alpha)
            wexp = jnp.repeat(jnp.transpose(gmlp_w[j][:, :ts, :ts], (1, 2, 0)).reshape(ts * ts, B_GROUPS), gd, axis=1)
            bexp = jnp.repeat(gmlp_b[j][:, :ts].T, gd, axis=1)
            prm_s = prm_p[:7] + (wexp, bexp) + prm_p[9:]
            xs, buf_s, gv = _even_sample(xs, tmaj(state_conv_a[:, j]), prm_s, alpha, ns, ts)
            outs["ca_p"].append(buf_p)
            outs["ca_s"].append(tmaj(buf_s))
            outs["gv_s"].append(tmaj(gv.reshape(ts, ns, w_b)))
        else:
            w_gates = jnp.concatenate([_block_diag(lru_wr[j]), _block_diag(lru_wi[j])], axis=1).astype(BF16)
            prm = (w_in_odd[j].astype(BF16), conv_c_w[j], _row(conv_c_b[j]), w_gates, _row(lru_br[j]),
                   _row(lru_bi[j]), _row(lru_lambda[j]))
            w_out = w_out_odd[j].astype(BF16)
            pg, pb = _row(post_g_odd[j]), _row(post_b_odd[j])
            c_out, q, k_t, v_t, ktb, vb, dg, kmean, cb_p, hl_p = _odd_proj_prompt(xp, prm, cos_p, sin_p)
            kmean = jnp.pad(kmean, ((0, 0), (0, nbp_p - nb_p), (0, 0)))
            d_out = _moba_prompt(q, ktb, vb, onehot, kmean, dg)
            xp = _out_proj(xp, c_out, d_out, w_out, pg, pb, alpha)
            outs["cc_p"].append(cb_p)
            outs["h_p"].append(hl_p[:, 0])
            outs["k_p"].append(time_major_heads(k_t))
            outs["v_p"].append(time_major_heads(v_t))
            c_out_s, q_s, k_s, v_s, dg_s, cb_s, hl_s = _odd_proj_sample(
                xs, tmaj(state_conv_c[:, j]), state_lru_h[:, j], prm, cos_s, sin_s, ns, ts)
            seq = lambda a: tmaj(a.reshape(ts, ns, wd))
            k_seq, v_seq = seq(k_s), seq(v_s)
            att = _paged_attn(page_table, seq(q_s), k_seq, v_seq, seq(dg_s), cache_k2, cache_v2, j)
            d_out_s = tmaj(att).reshape(ts * ns, wd)
            xs = _out_proj(xs[None], c_out_s[None], d_out_s[None], w_out, pg, pb, alpha)[0]
            outs["cc_s"].append(tmaj(cb_s))
            outs["h_s"].append(hl_s)
            outs["k_s"].append(k_seq.reshape(ns, ts, nh, hd))
            outs["v_s"].append(v_seq.reshape(ns, ts, nh, hd))

    st = lambda name: jnp.stack(outs[name], axis=1)
    return (xp, tmaj(xs.reshape(ts, ns, d)), st("ca_p"), st("ca_s"), st("gv_s"), st("cc_p"), st("cc_s"),
            st("h_p"), st("h_s"), st("k_p"), st("v_p"), st("k_s"), st("v_s"))
```

```python
import functools
import math

import jax
import jax.numpy as jnp
import numpy as np
from jax import lax
from jax.experimental import pallas as pl
from jax.experimental.pallas import tpu as pltpu

F32 = jnp.float32
BF16 = jnp.bfloat16

GMLP_CHUNK = 128
B_GROUPS = 4
C_BLOCKS = 8
LRU_C = 8.0
HD_D = 64
MOBA_BLOCK = 256
MOBA_TOPK = 3
ROPE_THETA = 10000.0
LN_EPS = 1e-5

LANES = 128
SUBLANES = 8
VMEM_LIMIT_BYTES = 56 * 1024 * 1024

MASK_NEG = -1e30
M_INIT = -1e29

PROMPT_TM = 512
CONV_HALO = 32
PAGES_PER_STEP = 32
MOBA_KEY_CHUNK = 1024
MOBA_Q_TILE = 1024


def _ln(x, g, b):
    mu = jnp.mean(x, axis=-1, keepdims=True)
    xc = x - mu
    var = jnp.mean(xc * xc, axis=-1, keepdims=True)
    return xc * lax.rsqrt(var + LN_EPS) * g + b


def _sigmoid(x):
    return jax.nn.sigmoid(x)


def _silu(x):
    return x * jax.nn.sigmoid(x)


def _gelu(x):
    c = math.sqrt(2.0 / math.pi)
    return x * (0.5 * (1.0 + jnp.tanh(c * (x + 0.044715 * (x * x * x)))))


def _softplus(y):
    return jnp.maximum(y, 0.0) + jnp.log1p(jnp.exp(-jnp.abs(y)))


def _dot(a, b):
    return jnp.dot(a, b, preferred_element_type=F32)


def _dot_nt(a, b, precision=None):
    return lax.dot_general(a, b, (((1,), (1,)), ((), ())), preferred_element_type=F32, precision=precision)


def _cparams(sem):
    return pltpu.CompilerParams(dimension_semantics=sem, vmem_limit_bytes=VMEM_LIMIT_BYTES)


def _full(shape):
    n = len(shape)
    return pl.BlockSpec(shape, lambda *_: (0,) * n)


def _zero_conv_halo(scr):
    scr[:, 0:CONV_HALO, :] = jnp.zeros((scr.shape[0], CONV_HALO, LANES), F32)


def _causal_conv_tile(scr, a_in, cw_ref, cb_ref, buf_ref, tm, kw, after_slab=None):
    w = a_in.shape[1]
    nl = w // LANES
    out = []
    for j in range(nl):
        lanes = slice(j * LANES, (j + 1) * LANES)
        scr[j, CONV_HALO:CONV_HALO + tm, :] = a_in[:, lanes]
        acc = jnp.zeros((tm, LANES), F32) + cb_ref[:, lanes]
        for k in range(kw):
            acc = acc + cw_ref[k:k + 1, lanes] * scr[j, pl.ds(CONV_HALO - (kw - 1) + k, tm), :]
        out.append(acc)
        buf_ref[0, :, lanes] = scr[j, pl.ds(CONV_HALO + tm - (kw - 1), kw - 1), :]
        scr[j, 0:CONV_HALO, :] = scr[j, pl.ds(tm, CONV_HALO), :]
        if after_slab is not None:
            after_slab(j)
    return jnp.concatenate(out, axis=1)


def _even_prompt_kernel(x_ref, win_ref, cw_ref, cb_ref, lag_ref, lab_ref, lvg_ref, lvb_ref, gmw_ref, gmb_ref,
                        wout_ref, pg_ref, pb_ref, y_ref, buf_ref, scr, *, alpha, tm, w, kw):
    @pl.when(pl.program_id(1) == 0)
    def _():
        _zero_conv_halo(scr)

    x = x_ref[0]
    z = _dot(x.astype(BF16), win_ref[...])
    a_val, a_glu, a_gate = z[:, 0:w], z[:, w:2 * w], z[:, 2 * w:3 * w]
    b_u, b_v, b_gate = z[:, 3 * w:4 * w], z[:, 4 * w:5 * w], z[:, 5 * w:6 * w]

    a_in = a_val * _sigmoid(a_glu)
    a_conv = _causal_conv_tile(scr, a_in, cw_ref, cb_ref, buf_ref, tm, kw)
    a_out = _silu(_ln(a_conv, lag_ref[...], lab_ref[...])) * _silu(a_gate)

    u = _gelu(b_u)
    v = _ln(_gelu(b_v), lvg_ref[...], lvb_ref[...]).astype(BF16)
    gd = w // B_GROUPS
    ri = lax.broadcasted_iota(jnp.int32, (GMLP_CHUNK, GMLP_CHUNK), 0)
    ci = lax.broadcasted_iota(jnp.int32, (GMLP_CHUNK, GMLP_CHUNK), 1)
    wmix = [jnp.where(ri >= ci, gmw_ref[g], 0.0).astype(BF16) for g in range(B_GROUPS)]
    rows = []
    for c in range(tm // GMLP_CHUNK):
        vc = v[c * GMLP_CHUNK:(c + 1) * GMLP_CHUNK, :]
        rows.append(jnp.concatenate(
            [_dot(wmix[g], vc[:, g * gd:(g + 1) * gd]) for g in range(B_GROUPS)], axis=1) + gmb_ref[...])
    mixed = jnp.concatenate(rows, axis=0)
    b_out = u * mixed * _silu(b_gate)

    y = _dot(a_out.astype(BF16), wout_ref[0:w, :]) + _dot(b_out.astype(BF16), wout_ref[w:2 * w, :])
    y_ref[0] = _ln(alpha * x + y, pg_ref[...], pb_ref[...])


def _even_prompt(x, prm, alpha):
    (w_in, conv_w, conv_b, ln_a_g, ln_a_b, ln_v_g, ln_v_b, gm_w, gm_bias, w_out, pg, pb) = prm
    bsz, t, d = x.shape
    tm = min(PROMPT_TM, t)
    w = conv_w.shape[1]
    kw = conv_w.shape[0]
    assert t % tm == 0 and tm % GMLP_CHUNK == 0 and kw - 1 <= CONV_HALO <= tm
    kern = functools.partial(_even_prompt_kernel, alpha=alpha, tm=tm, w=w, kw=kw)
    return pl.pallas_call(
        kern,
        grid=(bsz, t // tm),
        in_specs=[pl.BlockSpec((1, tm, d), lambda b, i: (b, i, 0)),
                  _full(w_in.shape), _full(conv_w.shape), _full(conv_b.shape), _full(ln_a_g.shape),
                  _full(ln_a_b.shape), _full(ln_v_g.shape), _full(ln_v_b.shape), _full(gm_w.shape),
                  _full(gm_bias.shape), _full(w_out.shape), _full(pg.shape), _full(pb.shape)],
        out_specs=[pl.BlockSpec((1, tm, d), lambda b, i: (b, i, 0)),
                   pl.BlockSpec((1, kw - 1, w), lambda b, i: (b, 0, 0))],
        out_shape=[jax.ShapeDtypeStruct((bsz, t, d), F32), jax.ShapeDtypeStruct((bsz, kw - 1, w), F32)],
        scratch_shapes=[pltpu.VMEM((w // LANES, CONV_HALO + tm, LANES), F32)],
        compiler_params=_cparams(("arbitrary", "arbitrary")),
        name="even_prompt",
    )(x, w_in, conv_w, conv_b, ln_a_g, ln_a_b, ln_v_g, ln_v_b, gm_w, gm_bias, w_out, pg, pb)


def _rope(x, cos, sin_signed):
    n = x.shape[1]
    reps = n // LANES
    cos_f = jnp.concatenate([cos] * reps, axis=1)
    sin_f = jnp.concatenate([sin_signed] * reps, axis=1)
    lane = lax.broadcasted_iota(jnp.int32, x.shape, 1)
    first = (lane % HD_D) < (HD_D // 2)
    rot = jnp.where(first, pltpu.roll(x, n - HD_D // 2, 1), pltpu.roll(x, HD_D // 2, 1))
    return x * cos_f + rot * sin_f


def _lru_coeffs(xc, wg_ref, br, bi, lam, w):
    gates = _dot(xc.astype(BF16), wg_ref[...])
    r = _sigmoid(gates[:, 0:w] + br)
    ig = _sigmoid(gates[:, w:2 * w] + bi)
    log_a = (-LRU_C) * r * _softplus(-lam)
    a = jnp.exp(log_a)
    bx = jnp.sqrt(1.0 - a * a) * (ig * xc)
    return a, bx


def _scan_tile(a, b, h0):
    tm = a.shape[0]
    row = lax.broadcasted_iota(jnp.int32, a.shape, 0) % SUBLANES
    s = 1
    while s < SUBLANES:
        a_sh = pltpu.roll(a, s, 0)
        b_sh = pltpu.roll(b, s, 0)
        ok = row >= s
        b = jnp.where(ok, a * b_sh + b, b)
        a = jnp.where(ok, a * a_sh, a)
        s *= 2
    h = h0
    hs = []
    for g in range(tm // SUBLANES):
        rows = slice(g * SUBLANES, (g + 1) * SUBLANES)
        hg = b[rows] + a[rows] * h
        hs.append(hg)
        h = hg[SUBLANES - 1:SUBLANES]
    return jnp.concatenate(hs, axis=0), h


def _odd_proj_prompt_kernel(x_ref, win_ref, cw_ref, cb_ref, wg_ref, br_ref, bi_ref, lam_ref, cos_ref, sin_ref,
                            cout_ref, q_ref, kt_ref, vt_ref, ktb_ref, vb_ref, dg_ref, kmean_ref, buf_ref, hl_ref,
                            scr, h_scr, *, tm, w, kw, scale):
    t = pl.program_id(1)

    @pl.when(t == 0)
    def _():
        _zero_conv_halo(scr)
        h_scr[...] = jnp.zeros_like(h_scr)

    x = x_ref[0]
    z = _dot(x.astype(BF16), win_ref[...])
    c_x, c_gate = z[:, 0:w], z[:, w:2 * w]
    wd = (z.shape[1] - 2 * w) // 4
    o = 2 * w
    q, k, v, d_gate = z[:, o:o + wd], z[:, o + wd:o + 2 * wd], z[:, o + 2 * wd:o + 3 * wd], z[:, o + 3 * wd:o + 4 * wd]

    xc = _causal_conv_tile(scr, c_x, cw_ref, cb_ref, buf_ref, tm, kw)
    a, bx = _lru_coeffs(xc, wg_ref, br_ref[...], bi_ref[...], lam_ref[...], w)
    h, h_last = _scan_tile(a, bx, h_scr[0:1, :])
    h_scr[0:1, :] = h_last
    hl_ref[0] = h_last
    cout_ref[0] = (h * _silu(c_gate)).astype(BF16)

    cos = cos_ref[...]
    sin = sin_ref[...]
    q_r = _rope(q, cos, sin)
    k_r = _rope(k, cos, sin)
    q_ref[0] = q_r * scale
    k_t = k_r.T
    kt_ref[0] = k_t
    vt_ref[0] = v.T
    vb_ref[0] = v.astype(BF16)
    dg_ref[0] = _silu(d_gate)
    ktb_ref[0, 0] = k_t.astype(BF16)
    nblk = tm // MOBA_BLOCK
    for c in range(nblk):
        kmean_ref[0, pl.ds(t * nblk + c, 1), :] = jnp.mean(
            k_r[c * MOBA_BLOCK:(c + 1) * MOBA_BLOCK, :], axis=0, keepdims=True)


def _odd_proj_prompt(x, prm, cos, sin):
    (w_in, conv_w, conv_b, w_gates, br, bi, lam) = prm
    bsz, t, d = x.shape
    tm = min(PROMPT_TM, t)
    w = conv_w.shape[1]
    kw = conv_w.shape[0]
    wd = (w_in.shape[1] - 2 * w) // 4
    nb = t // MOBA_BLOCK
    kc = min(MOBA_KEY_CHUNK, t)
    tpc = kc // tm
    assert t % tm == 0 and tm % MOBA_BLOCK == 0 and kw - 1 <= CONV_HALO <= tm and kc % tm == 0 and t % kc == 0
    kern = functools.partial(_odd_proj_prompt_kernel, tm=tm, w=w, kw=kw, scale=HD_D ** -0.5)
    row_blk = lambda n: pl.BlockSpec((1, tm, n), lambda b, i: (b, i, 0))
    per_b = lambda r, n: pl.BlockSpec((1, r, n), lambda b, i: (b, 0, 0))
    col_blk = pl.BlockSpec((1, wd, tm), lambda b, i: (b, 0, i))
    return pl.pallas_call(
        kern,
        grid=(bsz, t // tm),
        in_specs=[row_blk(d), _full(w_in.shape), _full(conv_w.shape), _full(conv_b.shape), _full(w_gates.shape),
                  _full(br.shape), _full(bi.shape), _full(lam.shape),
                  pl.BlockSpec((tm, LANES), lambda b, i: (i, 0)), pl.BlockSpec((tm, LANES), lambda b, i: (i, 0))],
        out_specs=[row_blk(w), row_blk(wd), col_blk, col_blk,
                   pl.BlockSpec((1, 1, wd, tm), lambda b, i: (b, i // tpc, 0, i % tpc)),
                   row_blk(wd), row_blk(wd), per_b(nb, wd), per_b(kw - 1, w), per_b(1, w)],
        out_shape=[jax.ShapeDtypeStruct((bsz, t, w), BF16),
                   jax.ShapeDtypeStruct((bsz, t, wd), F32),
                   jax.ShapeDtypeStruct((bsz, wd, t), F32),
                   jax.ShapeDtypeStruct((bsz, wd, t), F32),
                   jax.ShapeDtypeStruct((bsz, t // kc, wd, kc), BF16),
                   jax.ShapeDtypeStruct((bsz, t, wd), BF16),
                   jax.ShapeDtypeStruct((bsz, t, wd), F32),
                   jax.ShapeDtypeStruct((bsz, nb, wd), F32),
                   jax.ShapeDtypeStruct((bsz, kw - 1, w), F32),
                   jax.ShapeDtypeStruct((bsz, 1, w), F32)],
        scratch_shapes=[pltpu.VMEM((w // LANES, CONV_HALO + tm, LANES), F32), pltpu.VMEM((SUBLANES, w), F32)],
        compiler_params=_cparams(("arbitrary", "arbitrary")),
        name="odd_proj_prompt",
    )(x, w_in, conv_w, conv_b, w_gates, br, bi, lam, cos, sin)


def _select_topk_bias(gt, own):
    row = lax.broadcasted_iota(jnp.int32, gt.shape, 0)
    neg_inf = jnp.float32(-jnp.inf)
    g = jnp.where(row < own, gt, neg_inf)
    sel = row == own
    big = jnp.int32(gt.shape[0])
    for _ in range(MOBA_TOPK):
        mx = jnp.max(g, axis=0, keepdims=True)
        idx = jnp.min(jnp.where(g == mx, row, big), axis=0, keepdims=True)
        pick = (row == idx) & (mx > neg_inf)
        sel = sel | pick
        g = jnp.where(pick, neg_inf, g)
    return jnp.where(sel, 0.0, MASK_NEG)


def _moba_prompt_kernel(q_ref, kt_ref, v_ref, oh_ref, kmean_ref, dg_ref, o_ref, acc_ref, m_ref, *, nbp, nbr, kc, bq):
    g = pl.program_id(2)
    own = (g * bq + lax.broadcasted_iota(jnp.int32, (1, bq), 1)) // MOBA_BLOCK
    q = q_ref[0]
    lane = lax.broadcasted_iota(jnp.int32, (bq, LANES), 1)
    head_lanes = [lane < HD_D, lane >= HD_D]
    vlane = lax.broadcasted_iota(jnp.int32, (kc, LANES), 1)
    v_head = [vlane < HD_D, vlane >= HD_D]
    v_one = [(vlane == HD_D).astype(BF16), (vlane == 0).astype(BF16)]
    kml = lax.broadcasted_iota(jnp.int32, (nbr, LANES), 1)
    km_heads = [kml < HD_D, kml >= HD_D]
    kmean = kmean_ref[0, 0:nbr, :]

    q_hi = q.astype(BF16)
    q_lo = (q - q_hi.astype(F32)).astype(BF16)
    km = [jnp.where(km_heads[h], kmean, 0.0) for h in range(2)]
    km_hi = [x.astype(BF16) for x in km]
    km_lo = [(x - xh.astype(F32)).astype(BF16) for x, xh in zip(km, km_hi)]
    r_hi = _dot_nt(jnp.concatenate([km_hi[0], km_lo[0], km_hi[1], km_lo[1]], axis=0), q_hi)
    r_lo = _dot_nt(jnp.concatenate(km_hi, axis=0), q_lo)
    gts = [r_hi[2 * h * nbr:(2 * h + 1) * nbr] + r_hi[(2 * h + 1) * nbr:(2 * h + 2) * nbr]
           + r_lo[h * nbr:(h + 1) * nbr] for h in range(2)]
    bias_both = _select_topk_bias(jnp.concatenate(gts, axis=1), jnp.concatenate([own, own], axis=1))
    qaug = []
    for h in range(2):
        bias_t = bias_both[:, h * bq:(h + 1) * bq]
        if nbp > nbr:
            bias_t = jnp.concatenate([bias_t, jnp.full((nbp - nbr, bq), MASK_NEG, F32)], axis=0)
        qh = jnp.where(head_lanes[h], q, 0.0).astype(BF16)
        qaug.append(jnp.concatenate([qh, bias_t.T.astype(BF16)], axis=1))

    acc_ref[...] = jnp.zeros_like(acc_ref)
    m_ref[...] = jnp.full(m_ref.shape, M_INIT, F32)

    def chunk(c, causal):
        start = pl.multiple_of(c * kc, kc)
        vn = v_ref[0, pl.ds(start, kc), :]
        kaug = jnp.concatenate([kt_ref[0, c], oh_ref[c]], axis=0)
        if causal:
            qrow = lax.broadcasted_iota(jnp.int32, (bq, kc), 0) + (g * bq - c * kc)
            kcol = lax.broadcasted_iota(jnp.int32, (bq, kc), 1)
            causal_bias = jnp.where(kcol <= qrow, 0.0, MASK_NEG)
        for h in range(2):
            s = _dot(qaug[h], kaug)
            if causal:
                s = s + causal_bias
            m_prev = m_ref[h]
            m_new = jnp.maximum(m_prev, jnp.max(s, axis=1, keepdims=True))
            p = jnp.exp(s - jnp.concatenate([m_new] * (kc // LANES), axis=1))
            alpha = jnp.exp(m_prev - m_new)
            vh = jnp.where(v_head[h], vn, v_one[h])
            acc_ref[h] = alpha * acc_ref[h] + _dot(p.astype(BF16), vh)
            m_ref[h] = m_new

    last = ((g + 1) * bq - 1) // kc

    def body(c, carry):
        chunk(c, False)
        return carry

    lax.fori_loop(0, last, body, 0)
    chunk(last, True)

    acc_a = acc_ref[0]
    acc_b = acc_ref[1]
    out = jnp.where(head_lanes[0], acc_a / acc_a[:, HD_D:HD_D + 1], acc_b / acc_b[:, 0:1])
    o_ref[0] = (out * dg_ref[0]).astype(BF16)


def _moba_prompt(q, ktc, vb, onehot, kmean, dg):
    bsz, t, wd = q.shape
    nq = t // MOBA_BLOCK
    npairs = wd // LANES
    nbp = kmean.shape[1]
    nc, _, kc = ktc.shape[1:]
    nbr = -(-nq // SUBLANES) * SUBLANES
    bq = min(MOBA_Q_TILE, kc)
    assert t % MOBA_BLOCK == 0 and wd % LANES == 0 and nbp % LANES == 0 and nq <= nbp and kc % bq == 0
    assert bq % MOBA_BLOCK == 0
    kern = functools.partial(_moba_prompt_kernel, nbp=nbp, nbr=nbr, kc=kc, bq=bq)
    qblk = pl.BlockSpec((1, bq, LANES), lambda b, p, i: (b, i, p))
    seq = lambda r: pl.BlockSpec((1, r, LANES), lambda b, p, i: (b, 0, p))
    ktblk = pl.BlockSpec((1, nc, LANES, kc), lambda b, p, i: (b, 0, p, 0))
    return pl.pallas_call(
        kern,
        grid=(bsz, npairs, t // bq),
        in_specs=[qblk, ktblk, seq(t), _full(onehot.shape), seq(nbp), qblk],
        out_specs=qblk,
        out_shape=jax.ShapeDtypeStruct((bsz, t, wd), BF16),
        scratch_shapes=[pltpu.VMEM((2, bq, LANES), F32), pltpu.VMEM((2, bq, LANES), F32)],
        compiler_params=_cparams(("arbitrary", "arbitrary", "arbitrary")),
        name="moba_prompt",
    )(q, ktc, vb, onehot, kmean, dg)


def _out_proj_kernel(x_ref, c_ref, d_ref, wout_ref, pg_ref, pb_ref, y_ref, *, alpha, w):
    x = x_ref[0]
    y = _dot(c_ref[0], wout_ref[0:w, :]) + _dot(d_ref[0], wout_ref[w:, :])
    y_ref[0] = _ln(alpha * x + y, pg_ref[...], pb_ref[...])


def _out_proj(x, c_out, d_out, w_out, pg, pb, alpha):
    bsz, t, d = x.shape
    tm = min(PROMPT_TM, t)
    w = c_out.shape[2]
    assert t % tm == 0
    kern = functools.partial(_out_proj_kernel, alpha=alpha, w=w)
    blk = lambda n: pl.BlockSpec((1, tm, n), lambda b, i: (b, i, 0))
    return pl.pallas_call(
        kern,
        grid=(bsz, t // tm),
        in_specs=[blk(d), blk(w), blk(d_out.shape[2]), _full(w_out.shape), _full(pg.shape), _full(pb.shape)],
        out_specs=blk(d),
        out_shape=jax.ShapeDtypeStruct((bsz, t, d), F32),
        compiler_params=_cparams(("arbitrary", "arbitrary")),
        name="out_proj",
    )(x, c_out, d_out, w_out, pg, pb)


def _sample_conv(state_ref, a_in, cw_ref, cb, ns, ts, kw):
    def xp(s):
        return state_ref[s] if s < kw - 1 else a_in[(s - (kw - 1)) * ns:(s - (kw - 2)) * ns, :]

    out = []
    for t in range(ts):
        acc = jnp.zeros((ns, a_in.shape[1]), F32) + cb
        for k in range(kw):
            acc = acc + cw_ref[k:k + 1, :] * xp(t + k)
        out.append(acc)
    return jnp.concatenate(out, axis=0), [xp(ts + s) for s in range(kw - 1)]


def _even_sample_kernel(x_ref, st_ref, win_ref, cw_ref, cb_ref, lag_ref, lab_ref, lvg_ref, lvb_ref, wexp_ref,
                        bexp_ref, wout_ref, pg_ref, pb_ref, y_ref, buf_ref, gv_ref, *, alpha, ns, ts, w, kw):
    x = x_ref[...]
    z = _dot(x.astype(BF16), win_ref[...])
    a_val, a_glu, a_gate = z[:, 0:w], z[:, w:2 * w], z[:, 2 * w:3 * w]
    b_u, b_v, b_gate = z[:, 3 * w:4 * w], z[:, 4 * w:5 * w], z[:, 5 * w:6 * w]
    a_in = a_val * _sigmoid(a_glu)
    a_conv, new_state = _sample_conv(st_ref, a_in, cw_ref, cb_ref[...], ns, ts, kw)
    for s in range(kw - 1):
        buf_ref[s] = new_state[s]
    a_out = _silu(_ln(a_conv, lag_ref[...], lab_ref[...])) * _silu(a_gate)

    u = _gelu(b_u)
    v = _ln(_gelu(b_v), lvg_ref[...], lvb_ref[...])
    gv_ref[...] = v
    mixed = []
    for t in range(ts):
        acc = jnp.zeros((ns, w), F32)
        for s in range(t + 1):
            acc = acc + wexp_ref[t * ts + s:t * ts + s + 1, :] * v[s * ns:(s + 1) * ns, :]
        mixed.append(acc + bexp_ref[t:t + 1, :])
    b_out = u * jnp.concatenate(mixed, axis=0) * _silu(b_gate)
    y = _dot(a_out.astype(BF16), wout_ref[0:w, :]) + _dot(b_out.astype(BF16), wout_ref[w:2 * w, :])
    y_ref[...] = _ln(alpha * x + y, pg_ref[...], pb_ref[...])


def _even_sample(x, state, prm, alpha, ns, ts):
    (w_in, conv_w, conv_b, ln_a_g, ln_a_b, ln_v_g, ln_v_b, wexp, bexp, w_out, pg, pb) = prm
    r, d = x.shape
    w = conv_w.shape[1]
    kw = conv_w.shape[0]
    kern = functools.partial(_even_sample_kernel, alpha=alpha, ns=ns, ts=ts, w=w, kw=kw)
    args = (x, state, w_in, conv_w, conv_b, ln_a_g, ln_a_b, ln_v_g, ln_v_b, wexp, bexp, w_out, pg, pb)
    return pl.pallas_call(
        kern,
        grid=(1,),
        in_specs=[_full(a.shape) for a in args],
        out_specs=[_full((r, d)), _full((kw - 1, ns, w)), _full((r, w))],
        out_shape=[jax.ShapeDtypeStruct((r, d), F32), jax.ShapeDtypeStruct((kw - 1, ns, w), F32),
                   jax.ShapeDtypeStruct((r, w), F32)],
        compiler_params=_cparams(("arbitrary",)),
        name="even_sample",
    )(*args)


def _odd_proj_sample_kernel(x_ref, st_ref, h0_ref, win_ref, cw_ref, cb_ref, wg_ref, br_ref, bi_ref, lam_ref,
                            cos_ref, sin_ref, cout_ref, q_ref, k_ref, v_ref, dg_ref, buf_ref, hl_ref,
                            *, ns, ts, w, kw, scale):
    x = x_ref[...]
    z = _dot(x.astype(BF16), win_ref[...])
    c_x, c_gate = z[:, 0:w], z[:, w:2 * w]
    wd = (z.shape[1] - 2 * w) // 4
    o = 2 * w
    q, k, v, d_gate = z[:, o:o + wd], z[:, o + wd:o + 2 * wd], z[:, o + 2 * wd:o + 3 * wd], z[:, o + 3 * wd:o + 4 * wd]
    xc, new_state = _sample_conv(st_ref, c_x, cw_ref, cb_ref[...], ns, ts, kw)
    for s in range(kw - 1):
        buf_ref[s] = new_state[s]
    a, bx = _lru_coeffs(xc, wg_ref, br_ref[...], bi_ref[...], lam_ref[...], w)
    h = h0_ref[...]
    hs = []
    for t in range(ts):
        h = a[t * ns:(t + 1) * ns, :] * h + bx[t * ns:(t + 1) * ns, :]
        hs.append(h)
    hl_ref[...] = h
    cout_ref[...] = (jnp.concatenate(hs, axis=0) * _silu(c_gate)).astype(BF16)
    cos = cos_ref[...]
    sin = sin_ref[...]
    q_ref[...] = _rope(q, cos, sin) * scale
    k_ref[...] = _rope(k, cos, sin)
    v_ref[...] = v
    dg_ref[...] = _silu(d_gate)


def _odd_proj_sample(x, state, h0, prm, cos, sin, ns, ts):
    (w_in, conv_w, conv_b, w_gates, br, bi, lam) = prm
    r, d = x.shape
    w = conv_w.shape[1]
    kw = conv_w.shape[0]
    wd = (w_in.shape[1] - 2 * w) // 4
    kern = functools.partial(_odd_proj_sample_kernel, ns=ns, ts=ts, w=w, kw=kw, scale=HD_D ** -0.5)
    args = (x, state, h0, w_in, conv_w, conv_b, w_gates, br, bi, lam, cos, sin)
    sd = jax.ShapeDtypeStruct
    return pl.pallas_call(
        kern,
        grid=(1,),
        in_specs=[_full(a.shape) for a in args],
        out_specs=[_full((r, w)), _full((r, wd)), _full((r, wd)), _full((r, wd)), _full((r, wd)),
                   _full((kw - 1, ns, w)), _full((ns, w))],
        out_shape=[sd((r, w), BF16), sd((r, wd), F32), sd((r, wd), F32), sd((r, wd), F32), sd((r, wd), F32),
                   sd((kw - 1, ns, w), F32), sd((ns, w), F32)],
        compiler_params=_cparams(("arbitrary",)),
        name="odd_proj_sample",
    )(*args)


def _paged_attn_kernel(pt_ref, q_ref, kn_ref, vn_ref, dg_ref, *refs, nh, ts, pps, page, nblk, nbp, gb):
    k_refs = refs[0:pps]
    v_refs = refs[pps:2 * pps]
    o_ref = refs[2 * pps]
    gate_scr, m_scr, l_scr, o_scr = refs[2 * pps + 1:]
    s_idx = pl.program_id(1)
    nsteps = pl.num_programs(1)
    rows = ts * nh
    wd = q_ref.shape[2]
    ppb = MOBA_BLOCK // page
    bps = pps // ppb

    hm = lax.broadcasted_iota(jnp.int32, (nh, wd), 1) // HD_D == lax.broadcasted_iota(jnp.int32, (nh, wd), 0)
    q4 = q_ref[0]
    qexp = jnp.concatenate([jnp.where(hm, jnp.broadcast_to(q4[t:t + 1, :], (nh, wd)), 0.0) for t in range(ts)], axis=0)
    qexp_b = qexp.astype(BF16)
    q2 = jnp.concatenate([qexp_b, (qexp - qexp_b.astype(F32)).astype(BF16)], axis=0)

    @pl.when(s_idx == 0)
    def _():
        gate_scr[...] = jnp.zeros_like(gate_scr)
        m_scr[...] = jnp.zeros_like(m_scr)
        l_scr[...] = jnp.zeros_like(l_scr)

    g_lane = lax.broadcasted_iota(jnp.int32, (rows, nbp), 1)
    kt_all = jnp.concatenate([k_refs[j][0, 0] for j in range(pps)], axis=1).astype(BF16)
    s2_all = _dot(q2, kt_all)
    s_all = s2_all[0:rows] + s2_all[rows:2 * rows]
    for gi in range(bps // gb):
        p_ts = []
        for j in range(gb):
            c = gi * gb + j
            blk = s_idx * bps + c
            s = s_all[:, c * MOBA_BLOCK:(c + 1) * MOBA_BLOCK]
            m = jnp.max(s, axis=1, keepdims=True)
            p = jnp.exp(s - m).astype(BF16).astype(F32)
            gate_scr[...] = jnp.where(g_lane == blk, jnp.mean(s, axis=1, keepdims=True), gate_scr[...])
            m_scr[...] = jnp.where(g_lane == blk, m, m_scr[...])
            l_scr[...] = jnp.where(g_lane == blk, jnp.sum(p, axis=1, keepdims=True), l_scr[...])
            pieces = [jnp.zeros((j * rows, MOBA_BLOCK), F32)] * (j > 0) + [p]
            pieces += [jnp.zeros((LANES - (j + 1) * rows, MOBA_BLOCK), F32)] * (j < gb - 1)
            p_ts.append(jnp.concatenate(pieces, axis=0).T.astype(BF16))
        first = gi * gb * ppb
        vt_grp = jnp.concatenate([v_refs[first + j][0, 0] for j in range(gb * ppb)], axis=1)
        o_scr[s_idx * (bps // gb) + gi] = _dot(vt_grp.astype(BF16), jnp.concatenate(p_ts, axis=0))

    @pl.when(s_idx == nsteps - 1)
    def _():
        gates = gate_scr[...]
        lane = lax.broadcasted_iota(jnp.int32, gates.shape, 1)
        neg_inf = jnp.float32(-jnp.inf)
        g = jnp.where(lane < nblk, gates, neg_inf)
        sel = jnp.zeros(gates.shape, jnp.bool_)
        for _ in range(MOBA_TOPK):
            mx = jnp.max(g, axis=1, keepdims=True)
            idx = jnp.min(jnp.where(g == mx, lane, jnp.int32(nbp)), axis=1, keepdims=True)
            pick = (lane == idx) & (mx > neg_inf)
            sel = sel | pick
            g = jnp.where(pick, neg_inf, g)

        kn = kn_ref[0]
        vn = vn_ref[0]
        s_own = _dot_nt(qexp_b, kn.astype(BF16))
        tq = lax.broadcasted_iota(jnp.int32, s_own.shape, 0) // nh
        tk = lax.broadcasted_iota(jnp.int32, s_own.shape, 1)
        s_own = jnp.where(tk <= tq, s_own, MASK_NEG)
        m_own = jnp.max(s_own, axis=1, keepdims=True)
        m_blk = m_scr[...]
        m_tot = jnp.maximum(m_own, jnp.max(jnp.where(sel, m_blk, M_INIT), axis=1, keepdims=True))
        p_own = jnp.exp(s_own - m_tot)
        num = jnp.zeros((rows, wd), F32)
        for t in range(ts):
            num = num + p_own[:, t:t + 1] * vn[t:t + 1, :]
        wgt = jnp.where(sel, jnp.exp(jnp.where(sel, m_blk, M_INIT) - m_tot), 0.0)
        den = jnp.sum(p_own, axis=1, keepdims=True) + jnp.sum(wgt * l_scr[...], axis=1, keepdims=True)
        wgt_t = jnp.concatenate([wgt, jnp.zeros((LANES - rows, nbp), F32)], axis=0).T
        blk_row = lax.broadcasted_iota(jnp.int32, wgt_t.shape, 0)
        w_sh = jnp.where(blk_row % gb == 0, wgt_t, 0.0)
        for j in range(1, gb):
            w_sh = w_sh + jnp.where(blk_row % gb == j, pltpu.roll(wgt_t, j * rows, 1), 0.0)
        num_t = jnp.zeros((wd, LANES), F32)
        for g in range(nblk // gb):
            w_row = w_sh[g * gb:g * gb + 1, :]
            for j in range(1, gb):
                w_row = w_row + w_sh[g * gb + j:g * gb + j + 1, :]
            num_t = num_t + o_scr[g] * w_row
        num_g = num_t.T
        for j in range(gb):
            num = num + num_g[j * rows:(j + 1) * rows, :]
        out = num / den
        att = jnp.concatenate(
            [jnp.sum(jnp.where(hm, out[t * nh:(t + 1) * nh, :], 0.0), axis=0, keepdims=True) for t in range(ts)], axis=0)
        o_ref[0] = (att * dg_ref[0]).astype(BF16)


def _paged_attn(page_table, q, k_new, v_new, dg, cache_k, cache_v, layer):
    ns, ts, wd = q.shape
    npages = page_table.shape[1]
    page = cache_k.shape[3]
    nh = wd // HD_D
    pps = min(PAGES_PER_STEP, npages)
    ppb = MOBA_BLOCK // page
    assert MOBA_BLOCK % page == 0 and npages % pps == 0 and pps % ppb == 0
    nblk = npages // ppb
    nbp = -(-nblk // LANES) * LANES
    rows = ts * nh
    gb = LANES // rows
    assert LANES % rows == 0 and (pps // ppb) % gb == 0
    kern = functools.partial(_paged_attn_kernel, nh=nh, ts=ts, pps=pps, page=page, nblk=nblk, nbp=nbp, gb=gb)

    def page_spec(p):
        return pl.BlockSpec((1, 1, wd, page), lambda b, s, pt: (pt[b * npages + s * pps + p], layer, 0, 0))

    seq_spec = pl.BlockSpec((1, ts, wd), lambda b, s, pt: (b, 0, 0))
    rows = ts * nh
    grid_spec = pltpu.PrefetchScalarGridSpec(
        num_scalar_prefetch=1,
        grid=(ns, npages // pps),
        in_specs=[seq_spec, seq_spec, seq_spec, seq_spec] + [page_spec(p) for p in range(pps)] * 2,
        out_specs=seq_spec,
        scratch_shapes=[pltpu.VMEM((rows, nbp), F32), pltpu.VMEM((rows, nbp), F32), pltpu.VMEM((rows, nbp), F32),
                        pltpu.VMEM((nblk // gb, wd, LANES), F32)],
    )
    return pl.pallas_call(
        kern,
        grid_spec=grid_spec,
        out_shape=jax.ShapeDtypeStruct((ns, ts, wd), BF16),
        compiler_params=_cparams(("arbitrary", "arbitrary")),
        name="paged_attn",
    )(page_table.reshape(-1), q, k_new, v_new, dg, *([cache_k] * pps), *([cache_v] * pps))


def _rope_tables(pos):
    half = HD_D // 2
    inv = ROPE_THETA ** (-jnp.arange(half, dtype=F32) / half)
    ang = pos.astype(F32)[:, None] * inv[None, :]
    cos = jnp.cos(ang)
    sin = jnp.sin(ang)
    reps = LANES // HD_D
    return (jnp.concatenate([cos, cos] * reps, axis=1), jnp.concatenate([-sin, sin] * reps, axis=1))


def _block_diag(wb):
    n, d, _ = wb.shape
    eye = jnp.eye(n, dtype=wb.dtype)
    return (eye[:, None, :, None] * wb[:, :, None, :]).reshape(n * d, n * d)


def _row(v):
    return v.reshape(1, -1)


def kernel(x_prompt, x_sample, cache_k, cache_v, page_table, state_conv_a, state_conv_c, state_lru_h, w_in_even,
           conv_a_w, conv_a_b, ln_a_g, ln_a_b, ln_v_g, ln_v_b, gmlp_w, gmlp_b, w_out_even, post_g_even, post_b_even,
           w_in_odd, conv_c_w, conv_c_b, lru_wr, lru_br, lru_wi, lru_bi, lru_lambda, w_out_odd, post_g_odd,
           post_b_odd):
    n_even, n_odd = w_in_even.shape[0], w_in_odd.shape[0]
    depth = n_even + n_odd
    alpha = (2.0 * depth) ** 0.25
    bsz, t_p, d = x_prompt.shape
    ns, ts, _ = x_sample.shape
    n_pool, _, page, nh, hd = cache_k.shape
    assert hd == HD_D
    wd = nh * hd
    past_len = page_table.shape[1] * page
    assert past_len % MOBA_BLOCK == 0 and t_p % MOBA_BLOCK == 0
    w_b = ln_v_g.shape[1]
    gd = w_b // B_GROUPS

    cache_k2 = jnp.transpose(cache_k, (0, 1, 3, 4, 2)).reshape(n_pool, n_odd, wd, page)
    cache_v2 = jnp.transpose(cache_v, (0, 1, 3, 4, 2)).reshape(n_pool, n_odd, wd, page)
    time_major_heads = lambda a: jnp.transpose(a.reshape(bsz, nh, hd, t_p), (0, 3, 1, 2))
    cos_p, sin_p = _rope_tables(jnp.arange(t_p))
    cos_s, sin_s = _rope_tables(past_len + jnp.repeat(jnp.arange(ts), ns))
    nb_p = t_p // MOBA_BLOCK
    nbp_p = -(-nb_p // LANES) * LANES
    kc = min(MOBA_KEY_CHUNK, t_p)
    onehot = (jnp.arange(nbp_p)[None, :, None] == (jnp.arange(t_p) // MOBA_BLOCK).reshape(t_p // kc, 1, kc)).astype(BF16)

    xp = x_prompt
    xs = jnp.swapaxes(x_sample, 0, 1).reshape(ts * ns, d)
    tmaj = lambda a: jnp.swapaxes(a, 0, 1)
    outs = {k: [] for k in ("ca_p", "ca_s", "gv_s", "cc_p", "cc_s", "h_p", "h_s", "k_p", "v_p", "k_s", "v_s")}
    tc_s = min(ts, GMLP_CHUNK)
    assert ts % tc_s == 0 and ts == tc_s

    for l in range(depth):
        j = l // 2
        if l % 2 == 0:
            gm_bias_p = jnp.repeat(gmlp_b[j].T, gd, axis=1)
            prm_p = (w_in_even[j].astype(BF16), conv_a_w[j], _row(conv_a_b[j]), _row(ln_a_g[j]), _row(ln_a_b[j]),
                     _row(ln_v_g[j]), _row(ln_v_b[j]), gmlp_w[j], gm_bias_p, w_out_even[j].astype(BF16),
                     _row(post_g_even[j]), _row(post_b_even[j]))
            xp, buf_p = _even_prompt(xp, prm_p, alpha)
            wexp = jnp.repeat(jnp.transpose(gmlp_w[j][:, :ts, :ts], (1, 2, 0)).reshape(ts * ts, B_GROUPS), gd, axis=1)
            bexp = jnp.repeat(gmlp_b[j][:, :ts].T, gd, axis=1)
            prm_s = prm_p[:7] + (wexp, bexp) + prm_p[9:]
            xs, buf_s, gv = _even_sample(xs, tmaj(state_conv_a[:, j]), prm_s, alpha, ns, ts)
            outs["ca_p"].append(buf_p)
            outs["ca_s"].append(tmaj(buf_s))
            outs["gv_s"].append(tmaj(gv.reshape(ts, ns, w_b)))
        else:
            w_gates = jnp.concatenate([_block_diag(lru_wr[j]), _block_diag(lru_wi[j])], axis=1).astype(BF16)
            prm = (w_in_odd[j].astype(BF16), conv_c_w[j], _row(conv_c_b[j]), w_gates, _row(lru_br[j]),
                   _row(lru_bi[j]), _row(lru_lambda[j]))
            w_out = w_out_odd[j].astype(BF16)
            pg, pb = _row(post_g_odd[j]), _row(post_b_odd[j])
            c_out, q, k_t, v_t, ktb, vb, dg, kmean, cb_p, hl_p = _odd_proj_prompt(xp, prm, cos_p, sin_p)
            kmean = jnp.pad(kmean, ((0, 0), (0, nbp_p - nb_p), (0, 0)))
            d_out = _moba_prompt(q, ktb, vb, onehot, kmean, dg)
            xp = _out_proj(xp, c_out, d_out, w_out, pg, pb, alpha)
            outs["cc_p"].append(cb_p)
            outs["h_p"].append(hl_p[:, 0])
            outs["k_p"].append(time_major_heads(k_t))
            outs["v_p"].append(time_major_heads(v_t))
            c_out_s, q_s, k_s, v_s, dg_s, cb_s, hl_s = _odd_proj_sample(
                xs, tmaj(state_conv_c[:, j]), state_lru_h[:, j], prm, cos_s, sin_s, ns, ts)
            seq = lambda a: tmaj(a.reshape(ts, ns, wd))
            k_seq, v_seq = seq(k_s), seq(v_s)
            att = _paged_attn(page_table, seq(q_s), k_seq, v_seq, seq(dg_s), cache_k2, cache_v2, j)
            d_out_s = tmaj(att).reshape(ts * ns, wd)
            xs = _out_proj(xs[None], c_out_s[None], d_out_s[None], w_out, pg, pb, alpha)[0]
            outs["cc_s"].append(tmaj(cb_s))
            outs["h_s"].append(hl_s)
            outs["k_s"].append(k_seq.reshape(ns, ts, nh, hd))
            outs["v_s"].append(v_seq.reshape(ns, ts, nh, hd))

    st = lambda name: jnp.stack(outs[name], axis=1)
    return (xp, tmaj(xs.reshape(ts, ns, d)), st("ca_p"), st("ca_s"), st("gv_s"), st("cc_p"), st("cc_s"),
            st("h_p"), st("h_s"), st("k_p"), st("v_p"), st("k_s"), st("v_s"))
```

```python
import functools
import math

import jax
import jax.numpy as jnp
import numpy as np
from jax import lax
from jax.experimental import pallas as pl
from jax.experimental.pallas import tpu as pltpu

F32 = jnp.float32
BF16 = jnp.bfloat16

GMLP_CHUNK = 128
B_GROUPS = 4
C_BLOCKS = 8
LRU_C = 8.0
HD_D = 64
MOBA_BLOCK = 256
MOBA_TOPK = 3
ROPE_THETA = 10000.0
LN_EPS = 1e-5

LANES = 128
SUBLANES = 8
VMEM_LIMIT_BYTES = 56 * 1024 * 1024

MASK_NEG = -1e30
M_INIT = -1e29

PROMPT_TM = 512
CONV_HALO = 32
PAGES_PER_STEP = 32
MOBA_KEY_CHUNK = 1024
MOBA_Q_TILE = 1024


def _ln(x, g, b):
    mu = jnp.mean(x, axis=-1, keepdims=True)
    xc = x - mu
    var = jnp.mean(xc * xc, axis=-1, keepdims=True)
    return xc * lax.rsqrt(var + LN_EPS) * g + b


def _sigmoid(x):
    return jax.nn.sigmoid(x)


def _silu(x):
    return x * jax.nn.sigmoid(x)


def _gelu(x):
    c = math.sqrt(2.0 / math.pi)
    return x * (0.5 * (1.0 + jnp.tanh(c * (x + 0.044715 * (x * x * x)))))


def _softplus(y):
    return jnp.maximum(y, 0.0) + jnp.log1p(jnp.exp(-jnp.abs(y)))


def _dot(a, b):
    return jnp.dot(a, b, preferred_element_type=F32)


def _dot_nt(a, b, precision=None):
    return lax.dot_general(a, b, (((1,), (1,)), ((), ())), preferred_element_type=F32, precision=precision)


def _cparams(sem):
    return pltpu.CompilerParams(dimension_semantics=sem, vmem_limit_bytes=VMEM_LIMIT_BYTES)


def _full(shape):
    n = len(shape)
    return pl.BlockSpec(shape, lambda *_: (0,) * n)


def _zero_conv_halo(scr):
    scr[:, 0:CONV_HALO, :] = jnp.zeros((scr.shape[0], CONV_HALO, LANES), F32)


def _causal_conv_tile(scr, a_in, cw_ref, cb_ref, buf_ref, tm, kw, after_slab=None):
    w = a_in.shape[1]
    nl = w // LANES
    out = []
    for j in range(nl):
        lanes = slice(j * LANES, (j + 1) * LANES)
        scr[j, CONV_HALO:CONV_HALO + tm, :] = a_in[:, lanes]
        acc = jnp.zeros((tm, LANES), F32) + cb_ref[:, lanes]
        for k in range(kw):
            acc = acc + cw_ref[k:k + 1, lanes] * scr[j, pl.ds(CONV_HALO - (kw - 1) + k, tm), :]
        out.append(acc)
        buf_ref[0, :, lanes] = scr[j, pl.ds(CONV_HALO + tm - (kw - 1), kw - 1), :]
        scr[j, 0:CONV_HALO, :] = scr[j, pl.ds(tm, CONV_HALO), :]
        if after_slab is not None:
            after_slab(j)
    return jnp.concatenate(out, axis=1)


def _even_prompt_kernel(x_ref, win_ref, cw_ref, cb_ref, lag_ref, lab_ref, lvg_ref, lvb_ref, gmw_ref, gmb_ref,
                        wout_ref, pg_ref, pb_ref, y_ref, buf_ref, scr, *, alpha, tm, w, kw):
    @pl.when(pl.program_id(1) == 0)
    def _():
        _zero_conv_halo(scr)

    x = x_ref[0]
    z = _dot(x.astype(BF16), win_ref[...])
    a_val, a_glu, a_gate = z[:, 0:w], z[:, w:2 * w], z[:, 2 * w:3 * w]
    b_u, b_v, b_gate = z[:, 3 * w:4 * w], z[:, 4 * w:5 * w], z[:, 5 * w:6 * w]

    a_in = a_val * _sigmoid(a_glu)
    a_conv = _causal_conv_tile(scr, a_in, cw_ref, cb_ref, buf_ref, tm, kw)
    a_out = _silu(_ln(a_conv, lag_ref[...], lab_ref[...])) * _silu(a_gate)

    u = _gelu(b_u)
    v = _ln(_gelu(b_v), lvg_ref[...], lvb_ref[...]).astype(BF16)
    gd = w // B_GROUPS
    ri = lax.broadcasted_iota(jnp.int32, (GMLP_CHUNK, GMLP_CHUNK), 0)
    ci = lax.broadcasted_iota(jnp.int32, (GMLP_CHUNK, GMLP_CHUNK), 1)
    wmix = [jnp.where(ri >= ci, gmw_ref[g], 0.0).astype(BF16) for g in range(B_GROUPS)]
    rows = []
    for c in range(tm // GMLP_CHUNK):
        vc = v[c * GMLP_CHUNK:(c + 1) * GMLP_CHUNK, :]
        rows.append(jnp.concatenate(
            [_dot(wmix[g], vc[:, g * gd:(g + 1) * gd]) for g in range(B_GROUPS)], axis=1) + gmb_ref[...])
    mixed = jnp.concatenate(rows, axis=0)
    b_out = u * mixed * _silu(b_gate)

    y = _dot(a_out.astype(BF16), wout_ref[0:w, :]) + _dot(b_out.astype(BF16), wout_ref[w:2 * w, :])
    y_ref[0] = _ln(alpha * x + y, pg_ref[...], pb_ref[...])


def _even_prompt(x, prm, alpha):
    (w_in, conv_w, conv_b, ln_a_g, ln_a_b, ln_v_g, ln_v_b, gm_w, gm_bias, w_out, pg, pb) = prm
    bsz, t, d = x.shape
    tm = min(PROMPT_TM, t)
    w = conv_w.shape[1]
    kw = conv_w.shape[0]
    assert t % tm == 0 and tm % GMLP_CHUNK == 0 and kw - 1 <= CONV_HALO <= tm
    kern = functools.partial(_even_prompt_kernel, alpha=alpha, tm=tm, w=w, kw=kw)
    return pl.pallas_call(
        kern,
        grid=(bsz, t // tm),
        in_specs=[pl.BlockSpec((1, tm, d), lambda b, i: (b, i, 0)),
                  _full(w_in.shape), _full(conv_w.shape), _full(conv_b.shape), _full(ln_a_g.shape),
                  _full(ln_a_b.shape), _full(ln_v_g.shape), _full(ln_v_b.shape), _full(gm_w.shape),
                  _full(gm_bias.shape), _full(w_out.shape), _full(pg.shape), _full(pb.shape)],
        out_specs=[pl.BlockSpec((1, tm, d), lambda b, i: (b, i, 0)),
                   pl.BlockSpec((1, kw - 1, w), lambda b, i: (b, 0, 0))],
        out_shape=[jax.ShapeDtypeStruct((bsz, t, d), F32), jax.ShapeDtypeStruct((bsz, kw - 1, w), F32)],
        scratch_shapes=[pltpu.VMEM((w // LANES, CONV_HALO + tm, LANES), F32)],
        compiler_params=_cparams(("arbitrary", "arbitrary")),
        name="even_prompt",
    )(x, w_in, conv_w, conv_b, ln_a_g, ln_a_b, ln_v_g, ln_v_b, gm_w, gm_bias, w_out, pg, pb)


def _rope(x, cos, sin_signed):
    n = x.shape[1]
    reps = n // LANES
    cos_f = jnp.concatenate([cos] * reps, axis=1)
    sin_f = jnp.concatenate([sin_signed] * reps, axis=1)
    lane = lax.broadcasted_iota(jnp.int32, x.shape, 1)
    first = (lane % HD_D) < (HD_D // 2)
    rot = jnp.where(first, pltpu.roll(x, n - HD_D // 2, 1), pltpu.roll(x, HD_D // 2, 1))
    return x * cos_f + rot * sin_f


def _lru_coeffs(xc, wg_ref, br, bi, lam, w):
    gates = _dot(xc.astype(BF16), wg_ref[...])
    r = _sigmoid(gates[:, 0:w] + br)
    ig = _sigmoid(gates[:, w:2 * w] + bi)
    log_a = (-LRU_C) * r * _softplus(-lam)
    a = jnp.exp(log_a)
    bx = jnp.sqrt(1.0 - a * a) * (ig * xc)
    return a, bx


def _scan_tile(a, b, h0):
    tm = a.shape[0]
    row = lax.broadcasted_iota(jnp.int32, a.shape, 0) % SUBLANES
    s = 1
    while s < SUBLANES:
        a_sh = pltpu.roll(a, s, 0)
        b_sh = pltpu.roll(b, s, 0)
        ok = row >= s
        b = jnp.where(ok, a * b_sh + b, b)
        a = jnp.where(ok, a * a_sh, a)
        s *= 2
    h = h0
    hs = []
    for g in range(tm // SUBLANES):
        rows = slice(g * SUBLANES, (g + 1) * SUBLANES)
        hg = b[rows] + a[rows] * h
        hs.append(hg)
        h = hg[SUBLANES - 1:SUBLANES]
    return jnp.concatenate(hs, axis=0), h


def _odd_proj_prompt_kernel(x_ref, win_ref, cw_ref, cb_ref, wg_ref, br_ref, bi_ref, lam_ref, cos_ref, sin_ref,
                            *rest, tm, w, kw, scale, n_prev):
    if n_prev:
        kprev_ref, vprev_ref, *rest = rest
    (cout_ref, q_ref, kt_ref, vt_ref, ktb_ref, vb_ref, dg_ref, kmean_ref, buf_ref, hl_ref, scr, h_scr) = rest
    t = pl.program_id(1)

    @pl.when(t == 0)
    def _():
        _zero_conv_halo(scr)
        h_scr[...] = jnp.zeros_like(h_scr)

    x = x_ref[0]
    z = _dot(x.astype(BF16), win_ref[...])
    c_x, c_gate = z[:, 0:w], z[:, w:2 * w]
    wd = (z.shape[1] - 2 * w) // 4
    o = 2 * w
    q, k, v, d_gate = z[:, o:o + wd], z[:, o + wd:o + 2 * wd], z[:, o + 2 * wd:o + 3 * wd], z[:, o + 3 * wd:o + 4 * wd]

    xc = _causal_conv_tile(scr, c_x, cw_ref, cb_ref, buf_ref, tm, kw)
    a, bx = _lru_coeffs(xc, wg_ref, br_ref[...], bi_ref[...], lam_ref[...], w)
    h, h_last = _scan_tile(a, bx, h_scr[0:1, :])
    h_scr[0:1, :] = h_last
    hl_ref[0] = h_last
    cout_ref[0] = (h * _silu(c_gate)).astype(BF16)

    cos = cos_ref[...]
    sin = sin_ref[...]
    q_r = _rope(q, cos, sin)
    k_r = _rope(k, cos, sin)
    q_ref[0] = q_r * scale
    k_t = k_r.T
    if n_prev:
        kt_ref[0, 0:n_prev] = kprev_ref[0]
        vt_ref[0, 0:n_prev] = vprev_ref[0]
    kt_ref[0, n_prev] = k_t
    vt_ref[0, n_prev] = v.T
    vb_ref[0] = v.astype(BF16)
    dg_ref[0] = _silu(d_gate)
    ktb_ref[0, 0] = k_t.astype(BF16)
    nblk = tm // MOBA_BLOCK
    for c in range(nblk):
        kmean_ref[0, pl.ds(t * nblk + c, 1), :] = jnp.mean(
            k_r[c * MOBA_BLOCK:(c + 1) * MOBA_BLOCK, :], axis=0, keepdims=True)


def _odd_proj_prompt(x, prm, cos, sin, kv_prev=None):
    (w_in, conv_w, conv_b, w_gates, br, bi, lam) = prm
    bsz, t, d = x.shape
    n_prev = 0 if kv_prev is None else kv_prev[0].shape[1]
    tm = min(PROMPT_TM, t)
    w = conv_w.shape[1]
    kw = conv_w.shape[0]
    wd = (w_in.shape[1] - 2 * w) // 4
    nb = t // MOBA_BLOCK
    kc = min(MOBA_KEY_CHUNK, t)
    tpc = kc // tm
    assert t % tm == 0 and tm % MOBA_BLOCK == 0 and kw - 1 <= CONV_HALO <= tm and kc % tm == 0 and t % kc == 0
    kern = functools.partial(_odd_proj_prompt_kernel, tm=tm, w=w, kw=kw, scale=HD_D ** -0.5, n_prev=n_prev)
    row_blk = lambda n: pl.BlockSpec((1, tm, n), lambda b, i: (b, i, 0))
    per_b = lambda r, n: pl.BlockSpec((1, r, n), lambda b, i: (b, 0, 0))
    col_blk = lambda n: pl.BlockSpec((1, n, wd, tm), lambda b, i: (b, 0, 0, i))
    prev_specs = [col_blk(n_prev), col_blk(n_prev)] if n_prev else []
    return pl.pallas_call(
        kern,
        grid=(bsz, t // tm),
        in_specs=[row_blk(d), _full(w_in.shape), _full(conv_w.shape), _full(conv_b.shape), _full(w_gates.shape),
                  _full(br.shape), _full(bi.shape), _full(lam.shape),
                  pl.BlockSpec((tm, LANES), lambda b, i: (i, 0)), pl.BlockSpec((tm, LANES), lambda b, i: (i, 0))]
        + prev_specs,
        out_specs=[row_blk(w), row_blk(wd), col_blk(n_prev + 1), col_blk(n_prev + 1),
                   pl.BlockSpec((1, 1, wd, tm), lambda b, i: (b, i // tpc, 0, i % tpc)),
                   row_blk(wd), row_blk(wd), per_b(nb, wd), per_b(kw - 1, w), per_b(1, w)],
        out_shape=[jax.ShapeDtypeStruct((bsz, t, w), BF16),
                   jax.ShapeDtypeStruct((bsz, t, wd), F32),
                   jax.ShapeDtypeStruct((bsz, n_prev + 1, wd, t), F32),
                   jax.ShapeDtypeStruct((bsz, n_prev + 1, wd, t), F32),
                   jax.ShapeDtypeStruct((bsz, t // kc, wd, kc), BF16),
                   jax.ShapeDtypeStruct((bsz, t, wd), BF16),
                   jax.ShapeDtypeStruct((bsz, t, wd), F32),
                   jax.ShapeDtypeStruct((bsz, nb, wd), F32),
                   jax.ShapeDtypeStruct((bsz, kw - 1, w), F32),
                   jax.ShapeDtypeStruct((bsz, 1, w), F32)],
        scratch_shapes=[pltpu.VMEM((w // LANES, CONV_HALO + tm, LANES), F32), pltpu.VMEM((SUBLANES, w), F32)],
        compiler_params=_cparams(("arbitrary", "arbitrary")),
        name="odd_proj_prompt",
    )(x, w_in, conv_w, conv_b, w_gates, br, bi, lam, cos, sin, *(kv_prev or ()))


def _select_topk_bias(gt, own):
    row = lax.broadcasted_iota(jnp.int32, gt.shape, 0)
    neg_inf = jnp.float32(-jnp.inf)
    g = jnp.where(row < own, gt, neg_inf)
    sel = row == own
    big = jnp.int32(gt.shape[0])
    for _ in range(MOBA_TOPK):
        mx = jnp.max(g, axis=0, keepdims=True)
        idx = jnp.min(jnp.where(g == mx, row, big), axis=0, keepdims=True)
        pick = (row == idx) & (mx > neg_inf)
        sel = sel | pick
        g = jnp.where(pick, neg_inf, g)
    return jnp.where(sel, 0.0, MASK_NEG)


def _moba_prompt_kernel(q_ref, kt_ref, v_ref, oh_ref, kmean_ref, dg_ref, o_ref, acc_ref, m_ref, *, nbp, nbr, kc, bq):
    g = pl.program_id(2)
    own = (g * bq + lax.broadcasted_iota(jnp.int32, (1, bq), 1)) // MOBA_BLOCK
    q = q_ref[0]
    lane = lax.broadcasted_iota(jnp.int32, (bq, LANES), 1)
    head_lanes = [lane < HD_D, lane >= HD_D]
    vlane = lax.broadcasted_iota(jnp.int32, (kc, LANES), 1)
    v_head = [vlane < HD_D, vlane >= HD_D]
    v_one = [(vlane == HD_D).astype(BF16), (vlane == 0).astype(BF16)]
    kml = lax.broadcasted_iota(jnp.int32, (nbr, LANES), 1)
    km_heads = [kml < HD_D, kml >= HD_D]
    kmean = kmean_ref[0, 0:nbr, :]

    q_hi = q.astype(BF16)
    q_lo = (q - q_hi.astype(F32)).astype(BF16)
    km = [jnp.where(km_heads[h], kmean, 0.0) for h in range(2)]
    km_hi = [x.astype(BF16) for x in km]
    km_lo = [(x - xh.astype(F32)).astype(BF16) for x, xh in zip(km, km_hi)]
    r_hi = _dot_nt(jnp.concatenate([km_hi[0], km_lo[0], km_hi[1], km_lo[1]], axis=0), q_hi)
    r_lo = _dot_nt(jnp.concatenate(km_hi, axis=0), q_lo)
    gts = [r_hi[2 * h * nbr:(2 * h + 1) * nbr] + r_hi[(2 * h + 1) * nbr:(2 * h + 2) * nbr]
           + r_lo[h * nbr:(h + 1) * nbr] for h in range(2)]
    bias_both = _select_topk_bias(jnp.concatenate(gts, axis=1), jnp.concatenate([own, own], axis=1))
    qaug = []
    for h in range(2):
        bias_t = bias_both[:, h * bq:(h + 1) * bq]
        if nbp > nbr:
            bias_t = jnp.concatenate([bias_t, jnp.full((nbp - nbr, bq), MASK_NEG, F32)], axis=0)
        qh = jnp.where(head_lanes[h], q, 0.0).astype(BF16)
        qaug.append(jnp.concatenate([qh, bias_t.T.astype(BF16)], axis=1))

    acc_ref[...] = jnp.zeros_like(acc_ref)
    m_ref[...] = jnp.full(m_ref.shape, M_INIT, F32)

    def chunk(c, causal):
        start = pl.multiple_of(c * kc, kc)
        vn = v_ref[0, pl.ds(start, kc), :]
        kaug = jnp.concatenate([kt_ref[0, c], oh_ref[c]], axis=0)
        if causal:
            qrow = lax.broadcasted_iota(jnp.int32, (bq, kc), 0) + (g * bq - c * kc)
            kcol = lax.broadcasted_iota(jnp.int32, (bq, kc), 1)
            causal_bias = jnp.where(kcol <= qrow, 0.0, MASK_NEG)
        for h in range(2):
            s = _dot(qaug[h], kaug)
            if causal:
                s = s + causal_bias
            m_prev = m_ref[h]
            m_new = jnp.maximum(m_prev, jnp.max(s, axis=1, keepdims=True))
            p = jnp.exp(s - jnp.concatenate([m_new] * (kc // LANES), axis=1))
            alpha = jnp.exp(m_prev - m_new)
            vh = jnp.where(v_head[h], vn, v_one[h])
            acc_ref[h] = alpha * acc_ref[h] + _dot(p.astype(BF16), vh)
            m_ref[h] = m_new

    last = ((g + 1) * bq - 1) // kc

    def body(c, carry):
        chunk(c, False)
        return carry

    lax.fori_loop(0, last, body, 0)
    chunk(last, True)

    acc_a = acc_ref[0]
    acc_b = acc_ref[1]
    out = jnp.where(head_lanes[0], acc_a / acc_a[:, HD_D:HD_D + 1], acc_b / acc_b[:, 0:1])
    o_ref[0] = (out * dg_ref[0]).astype(BF16)


def _moba_prompt(q, ktc, vb, onehot, kmean, dg):
    bsz, t, wd = q.shape
    nq = t // MOBA_BLOCK
    npairs = wd // LANES
    nbp = kmean.shape[1]
    nc, _, kc = ktc.shape[1:]
    nbr = -(-nq // SUBLANES) * SUBLANES
    bq = min(MOBA_Q_TILE, kc)
    assert t % MOBA_BLOCK == 0 and wd % LANES == 0 and nbp % LANES == 0 and nq <= nbp and kc % bq == 0
    assert bq % MOBA_BLOCK == 0
    kern = functools.partial(_moba_prompt_kernel, nbp=nbp, nbr=nbr, kc=kc, bq=bq)
    qblk = pl.BlockSpec((1, bq, LANES), lambda b, p, i: (b, i, p))
    seq = lambda r: pl.BlockSpec((1, r, LANES), lambda b, p, i: (b, 0, p))
    ktblk = pl.BlockSpec((1, nc, LANES, kc), lambda b, p, i: (b, 0, p, 0))
    return pl.pallas_call(
        kern,
        grid=(bsz, npairs, t // bq),
        in_specs=[qblk, ktblk, seq(t), _full(onehot.shape), seq(nbp), qblk],
        out_specs=qblk,
        out_shape=jax.ShapeDtypeStruct((bsz, t, wd), BF16),
        scratch_shapes=[pltpu.VMEM((2, bq, LANES), F32), pltpu.VMEM((2, bq, LANES), F32)],
        compiler_params=_cparams(("arbitrary", "arbitrary", "arbitrary")),
        name="moba_prompt",
    )(q, ktc, vb, onehot, kmean, dg)


def _out_proj_kernel(x_ref, c_ref, d_ref, wout_ref, pg_ref, pb_ref, y_ref, *, alpha, w):
    x = x_ref[0]
    y = _dot(c_ref[0], wout_ref[0:w, :]) + _dot(d_ref[0], wout_ref[w:, :])
    y_ref[0] = _ln(alpha * x + y, pg_ref[...], pb_ref[...])


def _out_proj(x, c_out, d_out, w_out, pg, pb, alpha):
    bsz, t, d = x.shape
    tm = min(PROMPT_TM, t)
    w = c_out.shape[2]
    assert t % tm == 0
    kern = functools.partial(_out_proj_kernel, alpha=alpha, w=w)
    blk = lambda n: pl.BlockSpec((1, tm, n), lambda b, i: (b, i, 0))
    return pl.pallas_call(
        kern,
        grid=(bsz, t // tm),
        in_specs=[blk(d), blk(w), blk(d_out.shape[2]), _full(w_out.shape), _full(pg.shape), _full(pb.shape)],
        out_specs=blk(d),
        out_shape=jax.ShapeDtypeStruct((bsz, t, d), F32),
        compiler_params=_cparams(("arbitrary", "arbitrary")),
        name="out_proj",
    )(x, c_out, d_out, w_out, pg, pb)


def _sample_conv(state_ref, a_in, cw_ref, cb, ns, ts, kw):
    def xp(s):
        return state_ref[s] if s < kw - 1 else a_in[(s - (kw - 1)) * ns:(s - (kw - 2)) * ns, :]

    out = []
    for t in range(ts):
        acc = jnp.zeros((ns, a_in.shape[1]), F32) + cb
        for k in range(kw):
            acc = acc + cw_ref[k:k + 1, :] * xp(t + k)
        out.append(acc)
    return jnp.concatenate(out, axis=0), [xp(ts + s) for s in range(kw - 1)]


def _even_sample_kernel(x_ref, st_ref, win_ref, cw_ref, cb_ref, lag_ref, lab_ref, lvg_ref, lvb_ref, wexp_ref,
                        bexp_ref, wout_ref, pg_ref, pb_ref, y_ref, buf_ref, gv_ref, *, alpha, ns, ts, w, kw):
    x = x_ref[...]
    z = _dot(x.astype(BF16), win_ref[...])
    a_val, a_glu, a_gate = z[:, 0:w], z[:, w:2 * w], z[:, 2 * w:3 * w]
    b_u, b_v, b_gate = z[:, 3 * w:4 * w], z[:, 4 * w:5 * w], z[:, 5 * w:6 * w]
    a_in = a_val * _sigmoid(a_glu)
    a_conv, new_state = _sample_conv(st_ref, a_in, cw_ref, cb_ref[...], ns, ts, kw)
    for s in range(kw - 1):
        buf_ref[s] = new_state[s]
    a_out = _silu(_ln(a_conv, lag_ref[...], lab_ref[...])) * _silu(a_gate)

    u = _gelu(b_u)
    v = _ln(_gelu(b_v), lvg_ref[...], lvb_ref[...])
    gv_ref[...] = v
    mixed = []
    for t in range(ts):
        acc = jnp.zeros((ns, w), F32)
        for s in range(t + 1):
            acc = acc + wexp_ref[t * ts + s:t * ts + s + 1, :] * v[s * ns:(s + 1) * ns, :]
        mixed.append(acc + bexp_ref[t:t + 1, :])
    b_out = u * jnp.concatenate(mixed, axis=0) * _silu(b_gate)
    y = _dot(a_out.astype(BF16), wout_ref[0:w, :]) + _dot(b_out.astype(BF16), wout_ref[w:2 * w, :])
    y_ref[...] = _ln(alpha * x + y, pg_ref[...], pb_ref[...])


def _even_sample(x, state, prm, alpha, ns, ts):
    (w_in, conv_w, conv_b, ln_a_g, ln_a_b, ln_v_g, ln_v_b, wexp, bexp, w_out, pg, pb) = prm
    r, d = x.shape
    w = conv_w.shape[1]
    kw = conv_w.shape[0]
    kern = functools.partial(_even_sample_kernel, alpha=alpha, ns=ns, ts=ts, w=w, kw=kw)
    args = (x, state, w_in, conv_w, conv_b, ln_a_g, ln_a_b, ln_v_g, ln_v_b, wexp, bexp, w_out, pg, pb)
    return pl.pallas_call(
        kern,
        grid=(1,),
        in_specs=[_full(a.shape) for a in args],
        out_specs=[_full((r, d)), _full((kw - 1, ns, w)), _full((r, w))],
        out_shape=[jax.ShapeDtypeStruct((r, d), F32), jax.ShapeDtypeStruct((kw - 1, ns, w), F32),
                   jax.ShapeDtypeStruct((r, w), F32)],
        compiler_params=_cparams(("arbitrary",)),
        name="even_sample",
    )(*args)


def _odd_proj_sample_kernel(x_ref, st_ref, h0_ref, win_ref, cw_ref, cb_ref, wg_ref, br_ref, bi_ref, lam_ref,
                            cos_ref, sin_ref, cout_ref, q_ref, k_ref, v_ref, dg_ref, buf_ref, hl_ref,
                            *, ns, ts, w, kw, scale):
    x = x_ref[...]
    z = _dot(x.astype(BF16), win_ref[...])
    c_x, c_gate = z[:, 0:w], z[:, w:2 * w]
    wd = (z.shape[1] - 2 * w) // 4
    o = 2 * w
    q, k, v, d_gate = z[:, o:o + wd], z[:, o + wd:o + 2 * wd], z[:, o + 2 * wd:o + 3 * wd], z[:, o + 3 * wd:o + 4 * wd]
    xc, new_state = _sample_conv(st_ref, c_x, cw_ref, cb_ref[...], ns, ts, kw)
    for s in range(kw - 1):
        buf_ref[s] = new_state[s]
    a, bx = _lru_coeffs(xc, wg_ref, br_ref[...], bi_ref[...], lam_ref[...], w)
    h = h0_ref[...]
    hs = []
    for t in range(ts):
        h = a[t * ns:(t + 1) * ns, :] * h + bx[t * ns:(t + 1) * ns, :]
        hs.append(h)
    hl_ref[...] = h
    cout_ref[...] = (jnp.concatenate(hs, axis=0) * _silu(c_gate)).astype(BF16)
    cos = cos_ref[...]
    sin = sin_ref[...]
    q_ref[...] = _rope(q, cos, sin) * scale
    k_ref[...] = _rope(k, cos, sin)
    v_ref[...] = v
    dg_ref[...] = _silu(d_gate)


def _odd_proj_sample(x, state, h0, prm, cos, sin, ns, ts):
    (w_in, conv_w, conv_b, w_gates, br, bi, lam) = prm
    r, d = x.shape
    w = conv_w.shape[1]
    kw = conv_w.shape[0]
    wd = (w_in.shape[1] - 2 * w) // 4
    kern = functools.partial(_odd_proj_sample_kernel, ns=ns, ts=ts, w=w, kw=kw, scale=HD_D ** -0.5)
    args = (x, state, h0, w_in, conv_w, conv_b, w_gates, br, bi, lam, cos, sin)
    sd = jax.ShapeDtypeStruct
    return pl.pallas_call(
        kern,
        grid=(1,),
        in_specs=[_full(a.shape) for a in args],
        out_specs=[_full((r, w)), _full((r, wd)), _full((r, wd)), _full((r, wd)), _full((r, wd)),
                   _full((kw - 1, ns, w)), _full((ns, w))],
        out_shape=[sd((r, w), BF16), sd((r, wd), F32), sd((r, wd), F32), sd((r, wd), F32), sd((r, wd), F32),
                   sd((kw - 1, ns, w), F32), sd((ns, w), F32)],
        compiler_params=_cparams(("arbitrary",)),
        name="odd_proj_sample",
    )(*args)


def _paged_attn_kernel(pt_ref, q_ref, kn_ref, vn_ref, dg_ref, *refs, nh, ts, pps, page, nblk, nbp, gb):
    k_refs = refs[0:pps]
    v_refs = refs[pps:2 * pps]
    o_ref = refs[2 * pps]
    gate_scr, m_scr, l_scr, o_scr = refs[2 * pps + 1:]
    s_idx = pl.program_id(1)
    nsteps = pl.num_programs(1)
    rows = ts * nh
    wd = q_ref.shape[2]
    ppb = MOBA_BLOCK // page
    bps = pps // ppb

    hm = lax.broadcasted_iota(jnp.int32, (nh, wd), 1) // HD_D == lax.broadcasted_iota(jnp.int32, (nh, wd), 0)
    q4 = q_ref[0]
    qexp = jnp.concatenate([jnp.where(hm, jnp.broadcast_to(q4[t:t + 1, :], (nh, wd)), 0.0) for t in range(ts)], axis=0)
    qexp_b = qexp.astype(BF16)
    q2 = jnp.concatenate([qexp_b, (qexp - qexp_b.astype(F32)).astype(BF16)], axis=0)

    @pl.when(s_idx == 0)
    def _():
        gate_scr[...] = jnp.zeros_like(gate_scr)
        m_scr[...] = jnp.zeros_like(m_scr)
        l_scr[...] = jnp.zeros_like(l_scr)

    g_lane = lax.broadcasted_iota(jnp.int32, (rows, nbp), 1)
    kt_all = jnp.concatenate([k_refs[j][0, 0] for j in range(pps)], axis=1).astype(BF16)
    s2_all = _dot(q2, kt_all)
    s_all = s2_all[0:rows] + s2_all[rows:2 * rows]
    for gi in range(bps // gb):
        p_ts = []
        for j in range(gb):
            c = gi * gb + j
            blk = s_idx * bps + c
            s = s_all[:, c * MOBA_BLOCK:(c + 1) * MOBA_BLOCK]
            m = jnp.max(s, axis=1, keepdims=True)
            p = jnp.exp(s - m).astype(BF16).astype(F32)
            gate_scr[...] = jnp.where(g_lane == blk, jnp.mean(s, axis=1, keepdims=True), gate_scr[...])
            m_scr[...] = jnp.where(g_lane == blk, m, m_scr[...])
            l_scr[...] = jnp.where(g_lane == blk, jnp.sum(p, axis=1, keepdims=True), l_scr[...])
            pieces = [jnp.zeros((j * rows, MOBA_BLOCK), F32)] * (j > 0) + [p]
            pieces += [jnp.zeros((LANES - (j + 1) * rows, MOBA_BLOCK), F32)] * (j < gb - 1)
            p_ts.append(jnp.concatenate(pieces, axis=0).T.astype(BF16))
        first = gi * gb * ppb
        vt_grp = jnp.concatenate([v_refs[first + j][0, 0] for j in range(gb * ppb)], axis=1)
        o_scr[s_idx * (bps // gb) + gi] = _dot(vt_grp.astype(BF16), jnp.concatenate(p_ts, axis=0))

    @pl.when(s_idx == nsteps - 1)
    def _():
        gates = gate_scr[...]
        lane = lax.broadcasted_iota(jnp.int32, gates.shape, 1)
        neg_inf = jnp.float32(-jnp.inf)
        g = jnp.where(lane < nblk, gates, neg_inf)
        sel = jnp.zeros(gates.shape, jnp.bool_)
        for _ in range(MOBA_TOPK):
            mx = jnp.max(g, axis=1, keepdims=True)
            idx = jnp.min(jnp.where(g == mx, lane, jnp.int32(nbp)), axis=1, keepdims=True)
            pick = (lane == idx) & (mx > neg_inf)
            sel = sel | pick
            g = jnp.where(pick, neg_inf, g)

        kn = kn_ref[0]
        vn = vn_ref[0]
        s_own = _dot_nt(qexp_b, kn.astype(BF16))
        tq = lax.broadcasted_iota(jnp.int32, s_own.shape, 0) // nh
        tk = lax.broadcasted_iota(jnp.int32, s_own.shape, 1)
        s_own = jnp.where(tk <= tq, s_own, MASK_NEG)
        m_own = jnp.max(s_own, axis=1, keepdims=True)
        m_blk = m_scr[...]
        m_tot = jnp.maximum(m_own, jnp.max(jnp.where(sel, m_blk, M_INIT), axis=1, keepdims=True))
        p_own = jnp.exp(s_own - m_tot)
        num = jnp.zeros((rows, wd), F32)
        for t in range(ts):
            num = num + p_own[:, t:t + 1] * vn[t:t + 1, :]
        wgt = jnp.where(sel, jnp.exp(jnp.where(sel, m_blk, M_INIT) - m_tot), 0.0)
        den = jnp.sum(p_own, axis=1, keepdims=True) + jnp.sum(wgt * l_scr[...], axis=1, keepdims=True)
        wgt_t = jnp.concatenate([wgt, jnp.zeros((LANES - rows, nbp), F32)], axis=0).T
        blk_row = lax.broadcasted_iota(jnp.int32, wgt_t.shape, 0)
        w_sh = jnp.where(blk_row % gb == 0, wgt_t, 0.0)
        for j in range(1, gb):
            w_sh = w_sh + jnp.where(blk_row % gb == j, pltpu.roll(wgt_t, j * rows, 1), 0.0)
        num_t = jnp.zeros((wd, LANES), F32)
        for g in range(nblk // gb):
            w_row = w_sh[g * gb:g * gb + 1, :]
            for j in range(1, gb):
                w_row = w_row + w_sh[g * gb + j:g * gb + j + 1, :]
            num_t = num_t + o_scr[g] * w_row
        num_g = num_t.T
        for j in range(gb):
            num = num + num_g[j * rows:(j + 1) * rows, :]
        out = num / den
        att = jnp.concatenate(
            [jnp.sum(jnp.where(hm, out[t * nh:(t + 1) * nh, :], 0.0), axis=0, keepdims=True) for t in range(ts)], axis=0)
        o_ref[0] = (att * dg_ref[0]).astype(BF16)


def _paged_attn(page_table, q, k_new, v_new, dg, cache_k, cache_v, layer):
    ns, ts, wd = q.shape
    npages = page_table.shape[1]
    page = cache_k.shape[3]
    nh = wd // HD_D
    pps = min(PAGES_PER_STEP, npages)
    ppb = MOBA_BLOCK // page
    assert MOBA_BLOCK % page == 0 and npages % pps == 0 and pps % ppb == 0
    nblk = npages // ppb
    nbp = -(-nblk // LANES) * LANES
    rows = ts * nh
    gb = LANES // rows
    assert LANES % rows == 0 and (pps // ppb) % gb == 0
    kern = functools.partial(_paged_attn_kernel, nh=nh, ts=ts, pps=pps, page=page, nblk=nblk, nbp=nbp, gb=gb)

    def page_spec(p):
        return pl.BlockSpec((1, 1, wd, page), lambda b, s, pt: (pt[b * npages + s * pps + p], layer, 0, 0))

    seq_spec = pl.BlockSpec((1, ts, wd), lambda b, s, pt: (b, 0, 0))
    rows = ts * nh
    grid_spec = pltpu.PrefetchScalarGridSpec(
        num_scalar_prefetch=1,
        grid=(ns, npages // pps),
        in_specs=[seq_spec, seq_spec, seq_spec, seq_spec] + [page_spec(p) for p in range(pps)] * 2,
        out_specs=seq_spec,
        scratch_shapes=[pltpu.VMEM((rows, nbp), F32), pltpu.VMEM((rows, nbp), F32), pltpu.VMEM((rows, nbp), F32),
                        pltpu.VMEM((nblk // gb, wd, LANES), F32)],
    )
    return pl.pallas_call(
        kern,
        grid_spec=grid_spec,
        out_shape=jax.ShapeDtypeStruct((ns, ts, wd), BF16),
        compiler_params=_cparams(("arbitrary", "arbitrary")),
        name="paged_attn",
    )(page_table.reshape(-1), q, k_new, v_new, dg, *([cache_k] * pps), *([cache_v] * pps))


def _rope_tables(pos):
    half = HD_D // 2
    inv = ROPE_THETA ** (-jnp.arange(half, dtype=F32) / half)
    ang = pos.astype(F32)[:, None] * inv[None, :]
    cos = jnp.cos(ang)
    sin = jnp.sin(ang)
    reps = LANES // HD_D
    return (jnp.concatenate([cos, cos] * reps, axis=1), jnp.concatenate([-sin, sin] * reps, axis=1))


def _block_diag(wb):
    n, d, _ = wb.shape
    eye = jnp.eye(n, dtype=wb.dtype)
    return (eye[:, None, :, None] * wb[:, :, None, :]).reshape(n * d, n * d)


def _row(v):
    return v.reshape(1, -1)


def kernel(x_prompt, x_sample, cache_k, cache_v, page_table, state_conv_a, state_conv_c, state_lru_h, w_in_even,
           conv_a_w, conv_a_b, ln_a_g, ln_a_b, ln_v_g, ln_v_b, gmlp_w, gmlp_b, w_out_even, post_g_even, post_b_even,
           w_in_odd, conv_c_w, conv_c_b, lru_wr, lru_br, lru_wi, lru_bi, lru_lambda, w_out_odd, post_g_odd,
           post_b_odd):
    n_even, n_odd = w_in_even.shape[0], w_in_odd.shape[0]
    depth = n_even + n_odd
    alpha = (2.0 * depth) ** 0.25
    bsz, t_p, d = x_prompt.shape
    ns, ts, _ = x_sample.shape
    n_pool, _, page, nh, hd = cache_k.shape
    assert hd == HD_D
    wd = nh * hd
    past_len = page_table.shape[1] * page
    assert past_len % MOBA_BLOCK == 0 and t_p % MOBA_BLOCK == 0
    w_b = ln_v_g.shape[1]
    gd = w_b // B_GROUPS

    cache_k2 = jnp.transpose(cache_k, (0, 1, 3, 4, 2)).reshape(n_pool, n_odd, wd, page)
    cache_v2 = jnp.transpose(cache_v, (0, 1, 3, 4, 2)).reshape(n_pool, n_odd, wd, page)
    time_major_heads = lambda a: jnp.transpose(a.reshape(bsz, n_odd, nh, hd, t_p), (0, 1, 4, 2, 3))
    kv_prompt = None
    cos_p, sin_p = _rope_tables(jnp.arange(t_p))
    cos_s, sin_s = _rope_tables(past_len + jnp.repeat(jnp.arange(ts), ns))
    nb_p = t_p // MOBA_BLOCK
    nbp_p = -(-nb_p // LANES) * LANES
    kc = min(MOBA_KEY_CHUNK, t_p)
    onehot = (jnp.arange(nbp_p)[None, :, None] == (jnp.arange(t_p) // MOBA_BLOCK).reshape(t_p // kc, 1, kc)).astype(BF16)

    xp = x_prompt
    xs = jnp.swapaxes(x_sample, 0, 1).reshape(ts * ns, d)
    tmaj = lambda a: jnp.swapaxes(a, 0, 1)
    outs = {k: [] for k in ("ca_p", "ca_s", "gv_s", "cc_p", "cc_s", "h_p", "h_s", "k_p", "v_p", "k_s", "v_s")}
    tc_s = min(ts, GMLP_CHUNK)
    assert ts % tc_s == 0 and ts == tc_s

    for l in range(depth):
        j = l // 2
        if l % 2 == 0:
            gm_bias_p = jnp.repeat(gmlp_b[j].T, gd, axis=1)
            prm_p = (w_in_even[j].astype(BF16), conv_a_w[j], _row(conv_a_b[j]), _row(ln_a_g[j]), _row(ln_a_b[j]),
                     _row(ln_v_g[j]), _row(ln_v_b[j]), gmlp_w[j], gm_bias_p, w_out_even[j].astype(BF16),
                     _row(post_g_even[j]), _row(post_b_even[j]))
            xp, buf_p = _even_prompt(xp, prm_p, alpha)
            wexp = jnp.repeat(jnp.transpose(gmlp_w[j][:, :ts, :ts], (1, 2, 0)).reshape(ts * ts, B_GROUPS), gd, axis=1)
            bexp = jnp.repeat(gmlp_b[j][:, :ts].T, gd, axis=1)
            prm_s = prm_p[:7] + (wexp, bexp) + prm_p[9:]
            xs, buf_s, gv = _even_sample(xs, tmaj(state_conv_a[:, j]), prm_s, alpha, ns, ts)
            outs["ca_p"].append(buf_p)
            outs["ca_s"].append(tmaj(buf_s))
            outs["gv_s"].append(tmaj(gv.reshape(ts, ns, w_b)))
        else:
            w_gates = jnp.concatenate([_block_diag(lru_wr[j]), _block_diag(lru_wi[j])], axis=1).astype(BF16)
            prm = (w_in_odd[j].astype(BF16), conv_c_w[j], _row(conv_c_b[j]), w_gates, _row(lru_br[j]),
                   _row(lru_bi[j]), _row(lru_lambda[j]))
            w_out = w_out_odd[j].astype(BF16)
            pg, pb = _row(post_g_odd[j]), _row(post_b_odd[j])
            c_out, q, k_t, v_t, ktb, vb, dg, kmean, cb_p, hl_p = _odd_proj_prompt(xp, prm, cos_p, sin_p, kv_prompt)
            kv_prompt = (k_t, v_t)
            kmean = jnp.pad(kmean, ((0, 0), (0, nbp_p - nb_p), (0, 0)))
            d_out = _moba_prompt(q, ktb, vb, onehot, kmean, dg)
            xp = _out_proj(xp, c_out, d_out, w_out, pg, pb, alpha)
            outs["cc_p"].append(cb_p)
            outs["h_p"].append(hl_p[:, 0])
            c_out_s, q_s, k_s, v_s, dg_s, cb_s, hl_s = _odd_proj_sample(
                xs, tmaj(state_conv_c[:, j]), state_lru_h[:, j], prm, cos_s, sin_s, ns, ts)
            seq = lambda a: tmaj(a.reshape(ts, ns, wd))
            k_seq, v_seq = seq(k_s), seq(v_s)
            att = _paged_attn(page_table, seq(q_s), k_seq, v_seq, seq(dg_s), cache_k2, cache_v2, j)
            d_out_s = tmaj(att).reshape(ts * ns, wd)
            xs = _out_proj(xs[None], c_out_s[None], d_out_s[None], w_out, pg, pb, alpha)[0]
            outs["cc_s"].append(tmaj(cb_s))
            outs["h_s"].append(hl_s)
            outs["k_s"].append(k_seq.reshape(ns, ts, nh, hd))
            outs["v_s"].append(v_seq.reshape(ns, ts, nh, hd))

    st = lambda name: jnp.stack(outs[name], axis=1)
    k_p, v_p = (time_major_heads(a) for a in kv_prompt)
    return (xp, tmaj(xs.reshape(ts, ns, d)), st("ca_p"), st("ca_s"), st("gv_s"), st("cc_p"), st("cc_s"),
            st("h_p"), st("h_s"), k_p, v_p, st("k_s"), st("v_s"))
```
